```python
import jax, jax.numpy as jnp
from jax import lax
import numpy as np

D_MODEL = 2048
BATCH = 2
SEQ = 4096
DEPTH = 2

HEAD_DIM = 128
N_MLA_HEADS = 4
MLA_Q_LORA = 512
MLA_KV_LORA = 256
MLA_NOPE_DIM = 128
MLA_ROPE_DIM = 64
MLA_V_DIM = 128
N_MOBA_HEADS = 4
MOBA_BLOCK = 256
MOBA_TOPK = 3
MOBA_Q_CHUNK = 64
N_SWA_HEADS = 8
N_SWA_KV_HEADS = 2
SWA_WINDOW = 128
ATTN_Q_BLOCK = 128
D_FF = 5632
ROPE_THETA = 10000.0
NORM_EPS = 1e-6

MIX_WIDTH = N_MLA_HEADS * MLA_V_DIM + N_MOBA_HEADS * HEAD_DIM + N_SWA_HEADS * HEAD_DIM
IN_SIZES = (
    MLA_Q_LORA,
    MLA_KV_LORA,
    MLA_ROPE_DIM,
    N_MOBA_HEADS * HEAD_DIM,
    N_MOBA_HEADS * HEAD_DIM,
    N_MOBA_HEADS * HEAD_DIM,
    N_SWA_HEADS * HEAD_DIM,
    N_SWA_KV_HEADS * HEAD_DIM,
    N_SWA_KV_HEADS * HEAD_DIM,
)
IN_WIDTH = sum(IN_SIZES)

kernel_name = 'hymba_style_mla_moba_swa_macaron'


def rmsnorm(x, g):
    xf = x.astype(jnp.float32)
    y = xf * lax.rsqrt(jnp.mean(xf * xf, axis=-1, keepdims=True) + NORM_EPS)
    return (y * g.astype(jnp.float32)).astype(x.dtype)


def apply_rope(x, theta):
    S, d = x.shape[1], x.shape[-1]
    half = d // 2
    inv_freq = 1.0 / (theta ** (jnp.arange(half, dtype=jnp.float32) * (2.0 / d)))
    ang = jnp.arange(S, dtype=jnp.float32)[:, None] * inv_freq[None, :]
    cos = jnp.cos(ang)[None, :, None, :]
    sin = jnp.sin(ang)[None, :, None, :]
    xf = x.astype(jnp.float32)
    x1, x2 = xf[..., :half], xf[..., half:]
    return jnp.concatenate([x1 * cos - x2 * sin, x2 * cos + x1 * sin], axis=-1).astype(x.dtype)


def swiglu(x, w_gate, w_up, w_down):
    return (jax.nn.silu(x @ w_gate) * (x @ w_up)) @ w_down


def split_columns(z):
    cuts = np.cumsum(np.array(IN_SIZES))[:-1].tolist()
    return jnp.split(z, cuts, axis=-1)


def causal_attention_blocked(q, k, v, scale):
    B, S, H, dq = q.shape
    nb = S // ATTN_Q_BLOCK
    qb = q.reshape(B, nb, ATTN_Q_BLOCK, H, dq).transpose(1, 0, 2, 3, 4)
    kpos = jnp.arange(S)

    def one_block(args):
        qblk, i = args
        qpos = i * ATTN_Q_BLOCK + jnp.arange(ATTN_Q_BLOCK)
        s = jnp.einsum('bqhd,bkhd->bhqk', qblk, k).astype(jnp.float32) * scale
        s = jnp.where(kpos[None, :] <= qpos[:, None], s, -jnp.inf)
        p = jax.nn.softmax(s, axis=-1).astype(v.dtype)
        return jnp.einsum('bhqk,bkhd->bqhd', p, v)

    out = lax.map(one_block, (qb, jnp.arange(nb)))
    return out.transpose(1, 0, 2, 3, 4).reshape(B, S, H, v.shape[-1])


def mla_mixer(c_q, c_kv, k_rope, q_norm, w_uq, kv_norm, w_ukv):
    B, S, _ = c_q.shape
    H = N_MLA_HEADS
    q = (rmsnorm(c_q, q_norm) @ w_uq).reshape(B, S, H, MLA_NOPE_DIM + MLA_ROPE_DIM)
    q_nope, q_pe = q[..., :MLA_NOPE_DIM], q[..., MLA_NOPE_DIM:]
    q_pe = apply_rope(q_pe, ROPE_THETA)
    kv = (rmsnorm(c_kv, kv_norm) @ w_ukv).reshape(B, S, H, MLA_NOPE_DIM + MLA_V_DIM)
    k_nope, v = kv[..., :MLA_NOPE_DIM], kv[..., MLA_NOPE_DIM:]
    k_pe = apply_rope(k_rope[:, :, None, :], ROPE_THETA)
    k = jnp.concatenate([k_nope, jnp.broadcast_to(k_pe, (B, S, H, MLA_ROPE_DIM))], axis=-1)
    q = jnp.concatenate([q_nope, q_pe], axis=-1)
    o = causal_attention_blocked(q, k, v, (MLA_NOPE_DIM + MLA_ROPE_DIM) ** -0.5)
    return o.reshape(B, S, H * MLA_V_DIM)


def moba_attention(q, k, v):
    B, S, H, dh = q.shape
    Sp = -(-S // MOBA_BLOCK) * MOBA_BLOCK
    pad = Sp - S
    if pad:
        widths = ((0, 0), (0, pad), (0, 0), (0, 0))
        q, k, v = jnp.pad(q, widths), jnp.pad(k, widths), jnp.pad(v, widths)
    nkb = Sp // MOBA_BLOCK
    nqc = Sp // MOBA_Q_CHUNK
    top = min(MOBA_TOPK, nkb - 1)
    scale = dh ** -0.5
    kb = k.transpose(0, 2, 1, 3).reshape(B, H, nkb, MOBA_BLOCK, dh)
    vb = v.transpose(0, 2, 1, 3).reshape(B, H, nkb, MOBA_BLOCK, dh)
    kbar = jnp.mean(kb.astype(jnp.float32), axis=3).astype(k.dtype)
    qc = q.transpose(0, 2, 1, 3).reshape(B, H, nqc, MOBA_Q_CHUNK, dh).transpose(2, 0, 1, 3, 4)
    b_idx = jnp.arange(B)[:, None, None, None]
    h_idx = jnp.arange(H)[None, :, None, None]
    blk_ids = jnp.arange(nkb)

    def one_chunk(args):
        qblk, c = args
        qpos = c * MOBA_Q_CHUNK + jnp.arange(MOBA_Q_CHUNK)
        own = qpos[0] // MOBA_BLOCK
        k_own = lax.dynamic_index_in_dim(kb, own, axis=2, keepdims=False)
        v_own = lax.dynamic_index_in_dim(vb, own, axis=2, keepdims=False)
        s_own = jnp.einsum('bhqd,bhkd->bhqk', qblk, k_own).astype(jnp.float32) * scale
        kpos_own = own * MOBA_BLOCK + jnp.arange(MOBA_BLOCK)
        s_own = jnp.where(kpos_own[None, :] <= qpos[:, None], s_own, -jnp.inf)
        if top == 0:
            p = jax.nn.softmax(s_own, axis=-1).astype(v.dtype)
            return jnp.einsum('bhqk,bhkd->bhqd', p, v_own)
        gate = jnp.einsum('bhqd,bhnd->bhqn', qblk, kbar).astype(jnp.float32)
        gate = jnp.where(blk_ids < own, gate, -jnp.inf)
        _, sel = lax.top_k(gate, top)
        sel_valid = sel < own
        k_sel = kb[b_idx, h_idx, sel]
        v_sel = vb[b_idx, h_idx, sel]
        s_sel = jnp.einsum('bhqd,bhqjkd->bhqjk', qblk, k_sel).astype(jnp.float32) * scale
        s_sel = jnp.where(sel_valid[..., None], s_sel, -jnp.inf)
        s_sel = s_sel.reshape(B, H, MOBA_Q_CHUNK, top * MOBA_BLOCK)
        p = jax.nn.softmax(jnp.concatenate([s_sel, s_own], axis=-1), axis=-1).astype(v.dtype)
        p_sel = p[..., :top * MOBA_BLOCK].reshape(B, H, MOBA_Q_CHUNK, top, MOBA_BLOCK)
        p_own = p[..., top * MOBA_BLOCK:]
        return (jnp.einsum('bhqjk,bhqjkd->bhqd', p_sel, v_sel)
                + jnp.einsum('bhqk,bhkd->bhqd', p_own, v_own))

    out = lax.map(one_chunk, (qc, jnp.arange(nqc)))
    out = out.transpose(1, 0, 3, 2, 4).reshape(B, Sp, H * dh)
    return out[:, :S]


def swa_attention(q, k, v, sinks):
    B, S, Hq, dh = q.shape
    Hkv = k.shape[2]
    G = Hq // Hkv
    W = SWA_WINDOW
    nb = S // W
    scale = dh ** -0.5
    qb = q.reshape(B, nb, W, Hkv, G, dh)
    kb = k.reshape(B, nb, W, Hkv, dh)
    vb = v.reshape(B, nb, W, Hkv, dh)
    blk_pad = ((0, 0), (1, 0), (0, 0), (0, 0), (0, 0))
    kk = jnp.concatenate([jnp.pad(kb, blk_pad)[:, :-1], kb], axis=2)
    vv = jnp.concatenate([jnp.pad(vb, blk_pad)[:, :-1], vb], axis=2)
    s = jnp.einsum('bnqhgd,bnkhd->bhgnqk', qb, kk).astype(jnp.float32) * scale
    qpos = jnp.arange(nb)[:, None] * W + jnp.arange(W)[None, :]
    kpos = jnp.arange(nb)[:, None] * W - W + jnp.arange(2 * W)[None, :]
    rel = qpos[:, :, None] - kpos[:, None, :]
    mask = (rel >= 0) & (rel < W) & (kpos[:, None, :] >= 0)
    s = jnp.where(mask, s, -jnp.inf)
    sink = jnp.broadcast_to(sinks.astype(jnp.float32).reshape(1, Hkv, G, 1, 1, 1), s.shape[:-1] + (1,))
    p = jax.nn.softmax(jnp.concatenate([s, sink], axis=-1), axis=-1)[..., :-1].astype(v.dtype)
    o = jnp.einsum('bhgnqk,bnkhd->bnqhgd', p, vv)
    return o.reshape(B, S, Hq * dh)


def setup_inputs(seed: int = 0) -> dict:
    key = jax.random.key(seed)
    ks = jax.random.split(key, 24)
    L = DEPTH

    def normal(k, shape, fan_in):
        return jax.random.normal(k, shape, jnp.float32) * (fan_in ** -0.5)

    def gain(k, shape):
        return 1.0 + 0.01 * jax.random.normal(k, shape, jnp.float32)

    return {
        'x': jax.random.normal(ks[0], (BATCH, SEQ, D_MODEL), jnp.float32),
        'ffn1_norm': gain(ks[1], (L, D_MODEL)),
        'ffn1_w_gate': normal(ks[2], (L, D_MODEL, D_FF), D_MODEL),
        'ffn1_w_up': normal(ks[3], (L, D_MODEL, D_FF), D_MODEL),
        'ffn1_w_down': normal(ks[4], (L, D_FF, D_MODEL), D_FF),
        'attn_norm': gain(ks[5], (L, D_MODEL)),
        'w_in': normal(ks[6], (L, D_MODEL, IN_WIDTH), D_MODEL),
        'mla_q_norm': gain(ks[7], (L, MLA_Q_LORA)),
        'mla_w_uq': normal(ks[8], (L, MLA_Q_LORA, N_MLA_HEADS * (MLA_NOPE_DIM + MLA_ROPE_DIM)), MLA_Q_LORA),
        'mla_kv_norm': gain(ks[9], (L, MLA_KV_LORA)),
        'mla_w_ukv': normal(ks[10], (L, MLA_KV_LORA, N_MLA_HEADS * (MLA_NOPE_DIM + MLA_V_DIM)), MLA_KV_LORA),
        'swa_sinks': 0.5 * jax.random.normal(ks[11], (L, N_SWA_HEADS), jnp.float32),
        'w_out': normal(ks[12], (L, MIX_WIDTH, D_MODEL), MIX_WIDTH),
        'ffn2_norm': gain(ks[13], (L, D_MODEL)),
        'ffn2_w_gate': normal(ks[14], (L, D_MODEL, D_FF), D_MODEL),
        'ffn2_w_up': normal(ks[15], (L, D_MODEL, D_FF), D_MODEL),
        'ffn2_w_down': normal(ks[16], (L, D_FF, D_MODEL), D_FF),
        'final_norm': gain(ks[17], (D_MODEL,)),
    }


def reference(x, ffn1_norm, ffn1_w_gate, ffn1_w_up, ffn1_w_down, attn_norm, w_in,
              mla_q_norm, mla_w_uq, mla_kv_norm, mla_w_ukv, swa_sinks, w_out,
              ffn2_norm, ffn2_w_gate, ffn2_w_up, ffn2_w_down, final_norm):
    B, S, _ = x.shape
    for l in range(DEPTH):
        x = x + 0.5 * swiglu(rmsnorm(x, ffn1_norm[l]), ffn1_w_gate[l], ffn1_w_up[l], ffn1_w_down[l])

        h = rmsnorm(x, attn_norm[l])
        z = h @ w_in[l]
        c_q, c_kv, k_rope, mq, mk, mv, sq, sk, sv = split_columns(z)

        o_mla = mla_mixer(c_q, c_kv, k_rope, mla_q_norm[l], mla_w_uq[l], mla_kv_norm[l], mla_w_ukv[l])

        mq = apply_rope(mq.reshape(B, S, N_MOBA_HEADS, HEAD_DIM), ROPE_THETA)
        mk = apply_rope(mk.reshape(B, S, N_MOBA_HEADS, HEAD_DIM), ROPE_THETA)
        mv = mv.reshape(B, S, N_MOBA_HEADS, HEAD_DIM)
        o_moba = moba_attention(mq, mk, mv)

        sq = apply_rope(sq.reshape(B, S, N_SWA_HEADS, HEAD_DIM), ROPE_THETA)
        sk = apply_rope(sk.reshape(B, S, N_SWA_KV_HEADS, HEAD_DIM), ROPE_THETA)
        sv = sv.reshape(B, S, N_SWA_KV_HEADS, HEAD_DIM)
        o_swa = swa_attention(sq, sk, sv, swa_sinks[l])

        o = jnp.concatenate([o_mla, o_moba, o_swa], axis=-1)
        x = x + o @ w_out[l]

        x = x + 0.5 * swiglu(rmsnorm(x, ffn2_norm[l]), ffn2_w_gate[l], ffn2_w_up[l], ffn2_w_down[l])
    return rmsnorm(x, final_norm)
```

```python
import functools

import jax
import jax.numpy as jnp
import numpy as np
from jax import lax
from jax.experimental import pallas as pl
from jax.experimental.pallas import tpu as pltpu

D_MODEL = 2048
BATCH = 2
SEQ = 4096
DEPTH = 2
TOKENS = BATCH * SEQ

HEAD_DIM = 128
N_MLA_HEADS = 4
MLA_Q_LORA = 512
MLA_KV_LORA = 256
MLA_NOPE_DIM = 128
MLA_ROPE_DIM = 64
MLA_V_DIM = 128
MLA_QK_PAD = 256
N_MOBA_HEADS = 4
MOBA_BLOCK = 256
MOBA_TOPK = 3
N_SWA_HEADS = 8
N_SWA_KV_HEADS = 2
SWA_GROUP = N_SWA_HEADS // N_SWA_KV_HEADS
SWA_WINDOW = 128
D_FF = 5632
ROPE_THETA = 10000.0
NORM_EPS = 1e-6

LANES = 128
Z_WIDTH = 4096

Z_CQ = 0
Z_CKV = 512
Z_KROPE = 768
Z_MQ = 896
Z_MK = 1408
Z_MV = 1920
Z_SQ = 2432
Z_SK = 3456
Z_SV = 3712

VMEM_LIMIT = 56 * 1024 * 1024

BF16 = jnp.bfloat16
F32 = jnp.float32
NEG_INF = float("-inf")


def _params(*sem):
    return pltpu.CompilerParams(dimension_semantics=sem, vmem_limit_bytes=VMEM_LIMIT)


FFN_TM = 1024
FFN_TF = 256
NORM_ROWS = 128


def _rmsnorm_rows(x, g):
    ms = jnp.mean(x * x, axis=-1, keepdims=True)
    return x * lax.rsqrt(ms + NORM_EPS) * g


def _norm_into(x_ref, g_ref, h_ref, rows):
    g = g_ref[...]

    def body(r, carry):
        sl = pl.ds(pl.multiple_of(r * NORM_ROWS, NORM_ROWS), NORM_ROWS)
        h_ref[sl, :] = _rmsnorm_rows(x_ref[sl, :], g).astype(BF16)
        return carry

    lax.fori_loop(0, rows // NORM_ROWS, body, 0)


def _ffn_kernel(x_ref, g_ref, wg_ref, wu_ref, wd_ref, o_ref, h_ref):
    j = pl.program_id(1)

    @pl.when(j == 0)
    def _():
        _norm_into(x_ref, g_ref, h_ref, FFN_TM)

    h = h_ref[...]
    gate = jnp.dot(h, wg_ref[...].astype(BF16), preferred_element_type=F32)
    up = jnp.dot(h, wu_ref[...].astype(BF16), preferred_element_type=F32)
    act = (gate * jax.nn.sigmoid(gate) * up).astype(BF16)
    down = 0.5 * jnp.dot(act, wd_ref[...].astype(BF16), preferred_element_type=F32)

    @pl.when(j == 0)
    def _():
        o_ref[...] = x_ref[...] + down

    @pl.when(j > 0)
    def _():
        o_ref[...] += down


def _ffn(x, g, wg, wu, wd):
    grid = (TOKENS // FFN_TM, D_FF // FFN_TF)
    return pl.pallas_call(
        _ffn_kernel,
        grid=grid,
        in_specs=[
            pl.BlockSpec((FFN_TM, D_MODEL), lambda i, j: (i, 0)),
            pl.BlockSpec((1, D_MODEL), lambda i, j: (0, 0)),
            pl.BlockSpec((D_MODEL, FFN_TF), lambda i, j: (0, j)),
            pl.BlockSpec((D_MODEL, FFN_TF), lambda i, j: (0, j)),
            pl.BlockSpec((FFN_TF, D_MODEL), lambda i, j: (j, 0)),
        ],
        out_specs=pl.BlockSpec((FFN_TM, D_MODEL), lambda i, j: (i, 0)),
        out_shape=jax.ShapeDtypeStruct((TOKENS, D_MODEL), F32),
        scratch_shapes=[pltpu.VMEM((FFN_TM, D_MODEL), BF16)],
        compiler_params=_params("parallel", "arbitrary"),
        name="ffn",
    )(x, g.reshape(1, D_MODEL), wg, wu, wd)


PROJ_TM = 1024
PROJ_TN = 512


def _proj_kernel(x_ref, g_ref, w_ref, z_ref, h_ref):
    @pl.when(pl.program_id(1) == 0)
    def _():
        _norm_into(x_ref, g_ref, h_ref, PROJ_TM)

    z_ref[...] = jnp.dot(h_ref[...], w_ref[...], preferred_element_type=F32)


def _proj(x, g, w):
    grid = (TOKENS // PROJ_TM, Z_WIDTH // PROJ_TN)
    return pl.pallas_call(
        _proj_kernel,
        grid=grid,
        in_specs=[
            pl.BlockSpec((PROJ_TM, D_MODEL), lambda i, j: (i, 0)),
            pl.BlockSpec((1, D_MODEL), lambda i, j: (0, 0)),
            pl.BlockSpec((D_MODEL, PROJ_TN), lambda i, j: (0, j)),
        ],
        out_specs=pl.BlockSpec((PROJ_TM, PROJ_TN), lambda i, j: (i, j)),
        out_shape=jax.ShapeDtypeStruct((TOKENS, Z_WIDTH), F32),
        scratch_shapes=[pltpu.VMEM((PROJ_TM, D_MODEL), BF16)],
        compiler_params=_params("parallel", "arbitrary"),
        name="proj_in",
    )(x, g.reshape(1, D_MODEL), w)


PREP_TM = 512


def _rope(x, cos, sin_signed):
    return x * cos + pltpu.roll(x, LANES // 2, axis=1) * sin_signed


def _prep_kernel(z_ref, cos_ref, sin_ref, cosm_ref, sinm_ref, qn_ref, kvn_ref,
                 wuq_ref, wukv_ref,
                 mlaq_ref, mlak_ref, mlav_ref, mq_ref, mk_ref, mv_ref,
                 sq_ref, sk_ref, sv_ref):
    cos = cos_ref[...]
    sin = sin_ref[...]
    cosm = cosm_ref[...]
    sinm = sinm_ref[...]
    mla_scale = (MLA_NOPE_DIM + MLA_ROPE_DIM) ** -0.5
    scale = HEAD_DIM ** -0.5

    cq = _rmsnorm_rows(z_ref[:, Z_CQ:Z_CQ + MLA_Q_LORA], qn_ref[...]).astype(BF16)
    q = jnp.dot(cq, wuq_ref[...], preferred_element_type=F32)
    for hd in range(N_MLA_HEADS):
        base = hd * MLA_QK_PAD
        mlaq_ref[:, base:base + LANES] = (q[:, base:base + LANES] * mla_scale).astype(BF16)
        pe = _rope(q[:, base + LANES:base + 2 * LANES], cosm, sinm)
        mlaq_ref[:, base + LANES:base + 2 * LANES] = (pe * mla_scale).astype(BF16)

    ckv = _rmsnorm_rows(z_ref[:, Z_CKV:Z_CKV + MLA_KV_LORA], kvn_ref[...]).astype(BF16)
    kv = jnp.dot(ckv, wukv_ref[...], preferred_element_type=F32)
    kpe = _rope(z_ref[:, Z_KROPE:Z_KROPE + LANES], cosm, sinm).astype(BF16)
    for hd in range(N_MLA_HEADS):
        base = hd * MLA_QK_PAD
        mlak_ref[:, base:base + LANES] = kv[:, base:base + LANES].astype(BF16)
        mlak_ref[:, base + LANES:base + 2 * LANES] = kpe
        mlav_ref[:, hd * LANES:(hd + 1) * LANES] = kv[:, base + LANES:base + 2 * LANES].astype(BF16)

    for hd in range(N_MOBA_HEADS):
        c = hd * LANES
        mq_ref[:, c:c + LANES] = (_rope(z_ref[:, Z_MQ + c:Z_MQ + c + LANES], cos, sin) * scale).astype(BF16)
        mk_ref[:, c:c + LANES] = _rope(z_ref[:, Z_MK + c:Z_MK + c + LANES], cos, sin).astype(BF16)
    mv_ref[...] = z_ref[:, Z_MV:Z_MV + N_MOBA_HEADS * LANES].astype(BF16)

    for hd in range(N_SWA_HEADS):
        c = hd * LANES
        sq_ref[:, c:c + LANES] = (_rope(z_ref[:, Z_SQ + c:Z_SQ + c + LANES], cos, sin) * scale).astype(BF16)
    for hd in range(N_SWA_KV_HEADS):
        c = hd * LANES
        sk_ref[:, c:c + LANES] = _rope(z_ref[:, Z_SK + c:Z_SK + c + LANES], cos, sin).astype(BF16)
    sv_ref[...] = z_ref[:, Z_SV:Z_SV + N_SWA_KV_HEADS * LANES].astype(BF16)


def _prep(z, tabs, qn, kvn, wuq, wukv):
    nblk = SEQ // PREP_TM
    row = lambda i: (i, 0)
    tab = lambda i: (i % nblk, 0)
    const = lambda i: (0, 0)
    widths = [N_MLA_HEADS * MLA_QK_PAD, N_MLA_HEADS * MLA_QK_PAD, N_MLA_HEADS * MLA_V_DIM,
              N_MOBA_HEADS * HEAD_DIM, N_MOBA_HEADS * HEAD_DIM, N_MOBA_HEADS * HEAD_DIM,
              N_SWA_HEADS * HEAD_DIM, N_SWA_KV_HEADS * HEAD_DIM, N_SWA_KV_HEADS * HEAD_DIM]
    return pl.pallas_call(
        _prep_kernel,
        grid=(TOKENS // PREP_TM,),
        in_specs=[
            pl.BlockSpec((PREP_TM, Z_WIDTH), row),
            pl.BlockSpec((PREP_TM, LANES), tab),
            pl.BlockSpec((PREP_TM, LANES), tab),
            pl.BlockSpec((PREP_TM, LANES), tab),
            pl.BlockSpec((PREP_TM, LANES), tab),
            pl.BlockSpec((1, MLA_Q_LORA), const),
            pl.BlockSpec((1, MLA_KV_LORA), const),
            pl.BlockSpec((MLA_Q_LORA, N_MLA_HEADS * MLA_QK_PAD), const),
            pl.BlockSpec((MLA_KV_LORA, N_MLA_HEADS * MLA_QK_PAD), const),
        ],
        out_specs=[pl.BlockSpec((PREP_TM, w), row) for w in widths],
        out_shape=[jax.ShapeDtypeStruct((TOKENS, w), BF16) for w in widths],
        compiler_params=_params("parallel"),
        name="mixer_prep",
    )(z, *tabs, qn.reshape(1, -1), kvn.reshape(1, -1), wuq, wukv)


ATT_T = 256


def _split_bf16(x):
    hi = x.astype(BF16)
    lo = (x - hi.astype(F32)).astype(BF16)
    return hi, lo


def _attn_kernel(q_ref, k_ref, v_ref, o_ref, kbar_ref, *, moba):
    qi = pl.program_id(2)
    q = q_ref[...]
    nt = (((1,), (1,)), ((), ()))

    if moba:
        @pl.when(qi == 0)
        def _():
            def body(b, carry):
                sl = pl.ds(pl.multiple_of(b * ATT_T, ATT_T), ATT_T)
                kbar_ref[pl.ds(b, 1), :] = jnp.mean(k_ref[sl, :].astype(F32), axis=0, keepdims=True)
                return carry
            kbar_ref[...] = jnp.zeros_like(kbar_ref)
            lax.fori_loop(0, SEQ // ATT_T, body, 0)

        kb_hi, kb_lo = _split_bf16(kbar_ref[...])
        gate = (lax.dot_general(q, kb_hi, nt, preferred_element_type=F32)
                + lax.dot_general(q, kb_lo, nt, preferred_element_type=F32))
        lane = lax.broadcasted_iota(jnp.int32, gate.shape, 1)
        gate = jnp.where(lane < qi, gate, NEG_INF)
        sels = []
        for _ in range(MOBA_TOPK):
            top = jnp.max(gate, axis=-1, keepdims=True)
            idx = jnp.min(jnp.where(gate == top, lane, LANES), axis=-1, keepdims=True)
            sels.append(idx)
            gate = jnp.where(lane == idx, NEG_INF, gate)

    sl0 = pl.ds(pl.multiple_of(qi * ATT_T, ATT_T), ATT_T)
    s = lax.dot_general(q, k_ref[sl0, :], nt, preferred_element_type=F32)
    r = lax.broadcasted_iota(jnp.int32, s.shape, 0)
    c = lax.broadcasted_iota(jnp.int32, s.shape, 1)
    s = jnp.where(c <= r, s, NEG_INF)
    m0 = jnp.max(s, axis=-1, keepdims=True)
    p = jnp.exp(s - m0)
    l0 = jnp.sum(p, axis=-1, keepdims=True)
    acc0 = jnp.dot(p.astype(BF16), v_ref[sl0, :], preferred_element_type=F32)

    def body(j, carry):
        m, l, acc = carry
        sl = pl.ds(pl.multiple_of(j * ATT_T, ATT_T), ATT_T)
        s = lax.dot_general(q, k_ref[sl, :], nt, preferred_element_type=F32)
        if moba:
            keep = (sels[0] == j) | (sels[1] == j) | (sels[2] == j)
            s = jnp.where(keep, s, NEG_INF)
        m_new = jnp.maximum(m, jnp.max(s, axis=-1, keepdims=True))
        alpha = jnp.exp(m - m_new)
        p = jnp.exp(s - m_new)
        l = alpha * l + jnp.sum(p, axis=-1, keepdims=True)
        acc = alpha * acc + jnp.dot(p.astype(BF16), v_ref[sl, :], preferred_element_type=F32)
        return m_new, l, acc

    m, l, acc = lax.fori_loop(0, qi, body, (m0, l0, acc0))
    o_ref[...] = (acc / l).astype(o_ref.dtype)


def _attention(q, k, v, n_heads, dq, dv, moba):
    nq = SEQ // ATT_T
    return pl.pallas_call(
        functools.partial(_attn_kernel, moba=moba),
        grid=(BATCH, n_heads, nq),
        in_specs=[
            pl.BlockSpec((ATT_T, dq), lambda b, h, i: (b * nq + i, h)),
            pl.BlockSpec((SEQ, dq), lambda b, h, i: (b, h)),
            pl.BlockSpec((SEQ, dv), lambda b, h, i: (b, h)),
        ],
        out_specs=pl.BlockSpec((ATT_T, dv), lambda b, h, i: (b * nq + i, h)),
        out_shape=jax.ShapeDtypeStruct((TOKENS, n_heads * dv), BF16),
        scratch_shapes=[pltpu.VMEM((LANES, dq), F32)],
        compiler_params=_params("parallel", "parallel", "arbitrary"),
        name="moba_attn" if moba else "mla_attn",
    )(q, k, v)


def _swa_kernel(sink_ref, q_ref, kp_ref, kc_ref, vp_ref, vc_ref, o_ref):
    hk = pl.program_id(1)
    n = pl.program_id(2)
    W = SWA_WINDOW
    nt = (((1,), (1,)), ((), ()))
    kk = jnp.concatenate([kp_ref[...], kc_ref[...]], axis=0)
    vv = jnp.concatenate([vp_ref[...], vc_ref[...]], axis=0)
    r = lax.broadcasted_iota(jnp.int32, (W, 2 * W), 0)
    c = lax.broadcasted_iota(jnp.int32, (W, 2 * W), 1)
    rel = r + W - c
    mask = (rel >= 0) & (rel < W) & ((c >= W) | (n > 0))
    for g in range(SWA_GROUP):
        sink = sink_ref[hk * SWA_GROUP + g]
        s = lax.dot_general(q_ref[:, g * LANES:(g + 1) * LANES], kk, nt, preferred_element_type=F32)
        s = jnp.where(mask, s, NEG_INF)
        m = jnp.maximum(jnp.max(s, axis=-1, keepdims=True), sink)
        p = jnp.exp(s - m)
        denom = jnp.sum(p, axis=-1, keepdims=True) + jnp.exp(sink - m)
        o = jnp.dot(p.astype(BF16), vv, preferred_element_type=F32)
        o_ref[:, g * LANES:(g + 1) * LANES] = (o / denom).astype(o_ref.dtype)


def _swa(sinks, q, k, v):
    nb = SEQ // SWA_WINDOW
    W = SWA_WINDOW
    cur = lambda b, h, n: (b * nb + n, h)
    prev = lambda b, h, n: (b * nb + jnp.maximum(n - 1, 0), h)
    return pl.pallas_call(
        _swa_kernel,
        grid=(BATCH, N_SWA_KV_HEADS, nb),
        in_specs=[
            pl.BlockSpec(memory_space=pltpu.SMEM),
            pl.BlockSpec((W, SWA_GROUP * HEAD_DIM), cur),
            pl.BlockSpec((W, HEAD_DIM), prev),
            pl.BlockSpec((W, HEAD_DIM), cur),
            pl.BlockSpec((W, HEAD_DIM), prev),
            pl.BlockSpec((W, HEAD_DIM), cur),
        ],
        out_specs=pl.BlockSpec((W, SWA_GROUP * HEAD_DIM), cur),
        out_shape=jax.ShapeDtypeStruct((TOKENS, N_SWA_HEADS * HEAD_DIM), BF16),
        compiler_params=_params("parallel", "parallel", "arbitrary"),
        name="swa_attn",
    )(sinks, q, k, k, v, v)


OUT_TM = 512
W_MLA = N_MLA_HEADS * MLA_V_DIM
W_MOBA = N_MOBA_HEADS * HEAD_DIM
W_SWA = N_SWA_HEADS * HEAD_DIM


def _outproj_kernel(x_ref, a_ref, b_ref, c_ref, w_ref, o_ref):
    acc = jnp.dot(a_ref[...], w_ref[0:W_MLA, :], preferred_element_type=F32)
    acc += jnp.dot(b_ref[...], w_ref[W_MLA:W_MLA + W_MOBA, :], preferred_element_type=F32)
    acc += jnp.dot(c_ref[...], w_ref[W_MLA + W_MOBA:, :], preferred_element_type=F32)
    o_ref[...] = x_ref[...] + acc


def _outproj(x, a, b, c, w):
    row = lambda i: (i, 0)
    return pl.pallas_call(
        _outproj_kernel,
        grid=(TOKENS // OUT_TM,),
        in_specs=[
            pl.BlockSpec((OUT_TM, D_MODEL), row),
            pl.BlockSpec((OUT_TM, W_MLA), row),
            pl.BlockSpec((OUT_TM, W_MOBA), row),
            pl.BlockSpec((OUT_TM, W_SWA), row),
            pl.BlockSpec((D_MODEL, D_MODEL), lambda i: (0, 0)),
        ],
        out_specs=pl.BlockSpec((OUT_TM, D_MODEL), row),
        out_shape=jax.ShapeDtypeStruct((TOKENS, D_MODEL), F32),
        compiler_params=_params("parallel"),
        name="out_proj",
    )(x, a, b, c, w)


FIN_TM = 512


def _final_kernel(x_ref, g_ref, o_ref):
    o_ref[...] = _rmsnorm_rows(x_ref[...], g_ref[...])


def _final_norm(x, g):
    row = lambda i: (i, 0)
    return pl.pallas_call(
        _final_kernel,
        grid=(TOKENS // FIN_TM,),
        in_specs=[pl.BlockSpec((FIN_TM, D_MODEL), row), pl.BlockSpec((1, D_MODEL), lambda i: (0, 0))],
        out_specs=pl.BlockSpec((FIN_TM, D_MODEL), row),
        out_shape=jax.ShapeDtypeStruct((TOKENS, D_MODEL), F32),
        compiler_params=_params("parallel"),
        name="final_norm",
    )(x, g.reshape(1, D_MODEL))


def _rope_tables():
    def angles(d):
        half = d // 2
        inv_freq = 1.0 / (ROPE_THETA ** (jnp.arange(half, dtype=F32) * (2.0 / d)))
        return jnp.arange(SEQ, dtype=F32)[:, None] * inv_freq[None, :]

    ang = angles(HEAD_DIM)
    cos = jnp.concatenate([jnp.cos(ang), jnp.cos(ang)], axis=1)
    sin = jnp.concatenate([-jnp.sin(ang), jnp.sin(ang)], axis=1)
    angm = angles(MLA_ROPE_DIM)
    zero = jnp.zeros_like(angm)
    cosm = jnp.concatenate([jnp.cos(angm), zero, jnp.cos(angm), zero], axis=1)
    sinm = jnp.concatenate([-jnp.sin(angm), zero, jnp.sin(angm), zero], axis=1)
    return cos, sin, cosm, sinm


def _spread_rope_cols(w):
    half = MLA_ROPE_DIM // 2
    zero = jnp.zeros(w.shape[:-1] + (half,), w.dtype)
    return jnp.concatenate([w[..., :half], zero, w[..., half:], zero], axis=-1)


def _layout_w_in(w):
    cuts = np.cumsum([MLA_Q_LORA, MLA_KV_LORA, MLA_ROPE_DIM]).tolist()
    cq, ckv, krope, rest = jnp.split(w, cuts, axis=-1)
    pad = jnp.zeros((D_MODEL, Z_WIDTH - Z_SV - N_SWA_KV_HEADS * HEAD_DIM), w.dtype)
    return jnp.concatenate([cq, ckv, _spread_rope_cols(krope), rest, pad], axis=-1).astype(BF16)


def _layout_w_uq(w):
    w = w.reshape(MLA_Q_LORA, N_MLA_HEADS, MLA_NOPE_DIM + MLA_ROPE_DIM)
    w = jnp.concatenate([w[..., :MLA_NOPE_DIM], _spread_rope_cols(w[..., MLA_NOPE_DIM:])], axis=-1)
    return w.reshape(MLA_Q_LORA, N_MLA_HEADS * MLA_QK_PAD).astype(BF16)


def kernel(x, ffn1_norm, ffn1_w_gate, ffn1_w_up, ffn1_w_down, attn_norm, w_in, mla_q_norm, mla_w_uq, mla_kv_norm, mla_w_ukv, swa_sinks, w_out, ffn2_norm, ffn2_w_gate, ffn2_w_up, ffn2_w_down, final_norm):
    tabs = _rope_tables()
    x = x.reshape(TOKENS, D_MODEL)
    for l in range(DEPTH):
        x = _ffn(x, ffn1_norm[l], ffn1_w_gate[l], ffn1_w_up[l], ffn1_w_down[l])
        z = _proj(x, attn_norm[l], _layout_w_in(w_in[l]))
        (mla_q, mla_k, mla_v, mq, mk, mv, sq, sk, sv) = _prep(
            z, tabs, mla_q_norm[l], mla_kv_norm[l], _layout_w_uq(mla_w_uq[l]), mla_w_ukv[l].astype(BF16))
        o_mla = _attention(mla_q, mla_k, mla_v, N_MLA_HEADS, MLA_QK_PAD, MLA_V_DIM, moba=False)
        o_moba = _attention(mq, mk, mv, N_MOBA_HEADS, HEAD_DIM, HEAD_DIM, moba=True)
        o_swa = _swa(swa_sinks[l], sq, sk, sv)
        x = _outproj(x, o_mla, o_moba, o_swa, w_out[l].astype(BF16))
        x = _ffn(x, ffn2_norm[l], ffn2_w_gate[l], ffn2_w_up[l], ffn2_w_down[l])
    return _final_norm(x, final_norm).reshape(BATCH, SEQ, D_MODEL)
```

```python
import functools

import jax
import jax.numpy as jnp
import numpy as np
from jax import lax
from jax.experimental import pallas as pl
from jax.experimental.pallas import tpu as pltpu

D_MODEL = 2048
BATCH = 2
SEQ = 4096
DEPTH = 2
TOKENS = BATCH * SEQ

HEAD_DIM = 128
N_MLA_HEADS = 4
MLA_Q_LORA = 512
MLA_KV_LORA = 256
MLA_NOPE_DIM = 128
MLA_ROPE_DIM = 64
MLA_V_DIM = 128
MLA_QK_PAD = 256
N_MOBA_HEADS = 4
MOBA_BLOCK = 256
MOBA_TOPK = 3
MOBA_QK_AUG = 256
N_SWA_HEADS = 8
N_SWA_KV_HEADS = 2
SWA_GROUP = N_SWA_HEADS // N_SWA_KV_HEADS
SWA_WINDOW = 128
D_FF = 5632
ROPE_THETA = 10000.0
NORM_EPS = 1e-6

LANES = 128
Z_WIDTH = 4096

Z_CQ = 0
Z_CKV = 512
Z_KROPE = 768
Z_MQ = 896
Z_MK = 1408
Z_MV = 1920
Z_SQ = 2432
Z_SK = 3456
Z_SV = 3712

VMEM_LIMIT = 56 * 1024 * 1024

BF16 = jnp.bfloat16
F32 = jnp.float32
NEG_INF = float("-inf")


def _params(*sem, flags=None):
    return pltpu.CompilerParams(dimension_semantics=sem, vmem_limit_bytes=VMEM_LIMIT, flags=flags)


FFN_TM = 1024
FFN_TF = 256
FFN_DOWN_TN = 512
NORM_ROWS = 128


def _rmsnorm_rows(x, g):
    ms = jnp.mean(x * x, axis=-1, keepdims=True)
    return x * lax.rsqrt(ms + NORM_EPS) * g


def _norm_into(x_ref, g_ref, h_ref, rows, copy_ref=None):
    g = g_ref[...]

    def body(r, carry):
        sl = pl.ds(pl.multiple_of(r * NORM_ROWS, NORM_ROWS), NORM_ROWS)
        x = x_ref[sl, :]
        h_ref[sl, :] = _rmsnorm_rows(x, g).astype(BF16)
        if copy_ref is not None:
            copy_ref[sl, :] = x
        return carry

    lax.fori_loop(0, rows // NORM_ROWS, body, 0)


def _ffn_kernel(x_ref, g_ref, wg_ref, wu_ref, wd_ref, o_ref, h_ref):
    j = pl.program_id(1)

    @pl.when(j == 0)
    def _():
        _norm_into(x_ref, g_ref, h_ref, FFN_TM, copy_ref=o_ref)

    h = h_ref[...]
    gate = jnp.dot(h, wg_ref[...].astype(BF16), preferred_element_type=F32)
    up = jnp.dot(h, wu_ref[...].astype(BF16), preferred_element_type=F32)
    act = (gate * jax.nn.sigmoid(gate) * (0.5 * up)).astype(BF16)
    for c in range(D_MODEL // FFN_DOWN_TN):
        cols = slice(c * FFN_DOWN_TN, (c + 1) * FFN_DOWN_TN)
        o_ref[:, cols] += jnp.dot(act, wd_ref[:, cols].astype(BF16), preferred_element_type=F32)


def _ffn(x, g, wg, wu, wd, layer):
    grid = (TOKENS // FFN_TM, D_FF // FFN_TF)
    return pl.pallas_call(
        _ffn_kernel,
        grid=grid,
        in_specs=[
            pl.BlockSpec((FFN_TM, D_MODEL), lambda i, j: (i, 0)),
            pl.BlockSpec((None, 1, D_MODEL), lambda i, j: (layer, 0, 0)),
            pl.BlockSpec((None, D_MODEL, FFN_TF), lambda i, j: (layer, 0, j)),
            pl.BlockSpec((None, D_MODEL, FFN_TF), lambda i, j: (layer, 0, j)),
            pl.BlockSpec((None, FFN_TF, D_MODEL), lambda i, j: (layer, j, 0)),
        ],
        out_specs=pl.BlockSpec((FFN_TM, D_MODEL), lambda i, j: (i, 0)),
        out_shape=jax.ShapeDtypeStruct((TOKENS, D_MODEL), F32),
        scratch_shapes=[pltpu.VMEM((FFN_TM, D_MODEL), BF16)],
        compiler_params=_params("parallel", "arbitrary"),
        name="ffn",
    )(x, g, wg, wu, wd)


PROJ_TM = 1024
PROJ_TN = 512


def _proj_kernel(x_ref, g_ref, w_ref, z_ref, h_ref):
    @pl.when(pl.program_id(1) == 0)
    def _():
        _norm_into(x_ref, g_ref, h_ref, PROJ_TM)

    z_ref[...] = jnp.dot(h_ref[...], w_ref[...], preferred_element_type=F32)


def _proj(x, g, w):
    grid = (TOKENS // PROJ_TM, Z_WIDTH // PROJ_TN)
    return pl.pallas_call(
        _proj_kernel,
        grid=grid,
        in_specs=[
            pl.BlockSpec((PROJ_TM, D_MODEL), lambda i, j: (i, 0)),
            pl.BlockSpec((1, D_MODEL), lambda i, j: (0, 0)),
            pl.BlockSpec((D_MODEL, PROJ_TN), lambda i, j: (0, j)),
        ],
        out_specs=pl.BlockSpec((PROJ_TM, PROJ_TN), lambda i, j: (i, j)),
        out_shape=jax.ShapeDtypeStruct((TOKENS, Z_WIDTH), F32),
        scratch_shapes=[pltpu.VMEM((PROJ_TM, D_MODEL), BF16)],
        compiler_params=_params("parallel", "arbitrary"),
        name="proj_in",
    )(x, g.reshape(1, D_MODEL), w)


PREP_TM = 512


def _rope(x, cos, sin_signed):
    return x * cos + pltpu.roll(x, LANES // 2, axis=1) * sin_signed


def _prep_kernel(z_ref, cos_ref, sin_ref, cosm_ref, sinm_ref, qn_ref, kvn_ref,
                 wuq_ref, wukv_ref,
                 mlaq_ref, mlak_ref, mlav_ref, mq_ref, mk_ref, mv_ref,
                 sq_ref, sk_ref, sv_ref):
    cos = cos_ref[...]
    sin = sin_ref[...]
    cosm = cosm_ref[...]
    sinm = sinm_ref[...]
    mla_scale = (MLA_NOPE_DIM + MLA_ROPE_DIM) ** -0.5
    scale = HEAD_DIM ** -0.5

    cq = _rmsnorm_rows(z_ref[:, Z_CQ:Z_CQ + MLA_Q_LORA], qn_ref[...]).astype(BF16)
    q = jnp.dot(cq, wuq_ref[...], preferred_element_type=F32)
    for hd in range(N_MLA_HEADS):
        base = hd * MLA_QK_PAD
        mlaq_ref[:, base:base + LANES] = (q[:, base:base + LANES] * mla_scale).astype(BF16)
        pe = _rope(q[:, base + LANES:base + 2 * LANES], cosm, sinm)
        mlaq_ref[:, base + LANES:base + 2 * LANES] = (pe * mla_scale).astype(BF16)

    ckv = _rmsnorm_rows(z_ref[:, Z_CKV:Z_CKV + MLA_KV_LORA], kvn_ref[...]).astype(BF16)
    kv = jnp.dot(ckv, wukv_ref[...], preferred_element_type=F32)
    kpe = _rope(z_ref[:, Z_KROPE:Z_KROPE + LANES], cosm, sinm).astype(BF16)
    for hd in range(N_MLA_HEADS):
        base = hd * MLA_QK_PAD
        mlak_ref[:, base:base + LANES] = kv[:, base:base + LANES].astype(BF16)
        mlak_ref[:, base + LANES:base + 2 * LANES] = kpe
        mlav_ref[:, hd * LANES:(hd + 1) * LANES] = kv[:, base + LANES:base + 2 * LANES].astype(BF16)

    pos = (pl.program_id(0) % (SEQ // PREP_TM)) * PREP_TM + lax.broadcasted_iota(jnp.int32, (PREP_TM, LANES), 0)
    lane = lax.broadcasted_iota(jnp.int32, (PREP_TM, LANES), 1)
    block_onehot = jnp.where(pos // MOBA_BLOCK == lane, 1.0, 0.0).astype(BF16)
    for hd in range(N_MOBA_HEADS):
        c = hd * LANES
        mq_ref[:, c:c + LANES] = (_rope(z_ref[:, Z_MQ + c:Z_MQ + c + LANES], cos, sin) * scale).astype(BF16)
        mk_ref[:, 2 * c:2 * c + LANES] = _rope(z_ref[:, Z_MK + c:Z_MK + c + LANES], cos, sin).astype(BF16)
        mk_ref[:, 2 * c + LANES:2 * c + 2 * LANES] = block_onehot
    mv_ref[...] = z_ref[:, Z_MV:Z_MV + N_MOBA_HEADS * LANES].astype(BF16)

    for hd in range(N_SWA_HEADS):
        c = hd * LANES
        sq_ref[:, c:c + LANES] = (_rope(z_ref[:, Z_SQ + c:Z_SQ + c + LANES], cos, sin) * scale).astype(BF16)
    for hd in range(N_SWA_KV_HEADS):
        c = hd * LANES
        sk_ref[:, c:c + LANES] = _rope(z_ref[:, Z_SK + c:Z_SK + c + LANES], cos, sin).astype(BF16)
    sv_ref[...] = z_ref[:, Z_SV:Z_SV + N_SWA_KV_HEADS * LANES].astype(BF16)


def _prep(z, tabs, qn, kvn, wuq, wukv):
    nblk = SEQ // PREP_TM
    row = lambda i: (i, 0)
    tab = lambda i: (i % nblk, 0)
    const = lambda i: (0, 0)
    widths = [N_MLA_HEADS * MLA_QK_PAD, N_MLA_HEADS * MLA_QK_PAD, N_MLA_HEADS * MLA_V_DIM,
              N_MOBA_HEADS * HEAD_DIM, N_MOBA_HEADS * MOBA_QK_AUG, N_MOBA_HEADS * HEAD_DIM,
              N_SWA_HEADS * HEAD_DIM, N_SWA_KV_HEADS * HEAD_DIM, N_SWA_KV_HEADS * HEAD_DIM]
    return pl.pallas_call(
        _prep_kernel,
        grid=(TOKENS // PREP_TM,),
        in_specs=[
            pl.BlockSpec((PREP_TM, Z_WIDTH), row),
            pl.BlockSpec((PREP_TM, LANES), tab),
            pl.BlockSpec((PREP_TM, LANES), tab),
            pl.BlockSpec((PREP_TM, LANES), tab),
            pl.BlockSpec((PREP_TM, LANES), tab),
            pl.BlockSpec((1, MLA_Q_LORA), const),
            pl.BlockSpec((1, MLA_KV_LORA), const),
            pl.BlockSpec((MLA_Q_LORA, N_MLA_HEADS * MLA_QK_PAD), const),
            pl.BlockSpec((MLA_KV_LORA, N_MLA_HEADS * MLA_QK_PAD), const),
        ],
        out_specs=[pl.BlockSpec((PREP_TM, w), row) for w in widths],
        out_shape=[jax.ShapeDtypeStruct((TOKENS, w), BF16) for w in widths],
        compiler_params=_params("parallel"),
        name="mixer_prep",
    )(z, *tabs, qn.reshape(1, -1), kvn.reshape(1, -1), wuq, wukv)


ATT_TQ = MOBA_BLOCK
ATT_TK = 2 * MOBA_BLOCK


def _split_bf16(x):
    hi = x.astype(BF16)
    lo = (x - hi.astype(F32)).astype(BF16)
    return hi, lo


MASKED = -1e30


def _moba_gate_logits(q, kbar, qi):
    nblk = SEQ // MOBA_BLOCK
    nt = (((1,), (1,)), ((), ()))
    kb_hi, kb_lo = _split_bf16(kbar)
    gate = (lax.dot_general(kb_hi, q, nt, preferred_element_type=F32)
            + lax.dot_general(kb_lo, q, nt, preferred_element_type=F32))[:nblk]
    blk = lax.broadcasted_iota(jnp.int32, gate.shape, 0)
    gate = jnp.where(blk < qi, gate, NEG_INF)
    rank = jnp.zeros(gate.shape, jnp.int32)
    for other in range(nblk):
        row = gate[other:other + 1, :]
        beats = (row > gate) | ((row == gate) & (other < blk))
        rank = rank + jnp.where(beats, 1, 0)
    keep = ((rank < MOBA_TOPK) & (blk < qi)) | (blk == qi)
    logit_t = jnp.where(keep, 0.0, MASKED)
    logit_t = jnp.concatenate([logit_t, jnp.full((LANES - nblk, gate.shape[1]), MASKED, F32)], axis=0)
    return logit_t.T


def _attn_kernel(q_ref, k_ref, v_ref, o_ref, m_ref, l_ref, acc_ref, *moba_scratch, n_heads, dq, dv, moba):
    qi = pl.program_id(1)
    nt = (((1,), (1,)), ((), ()))

    if moba:
        kbar_ref, qaug_ref = moba_scratch

        @pl.when(qi == 0)
        def _():
            kbar_ref[...] = jnp.zeros_like(kbar_ref)

            def body(b, carry):
                sl = pl.ds(pl.multiple_of(b * MOBA_BLOCK, MOBA_BLOCK), MOBA_BLOCK)
                kbar_ref[pl.ds(b, 1), :] = jnp.mean(k_ref[sl, :].astype(F32), axis=0, keepdims=True)
                return carry

            lax.fori_loop(0, SEQ // MOBA_BLOCK, body, 0)

        for hd in range(n_heads):
            qh = q_ref[:, hd * HEAD_DIM:(hd + 1) * HEAD_DIM]
            logits = _moba_gate_logits(qh, kbar_ref[:, hd * dq:hd * dq + HEAD_DIM], qi)
            qaug_ref[:, hd * dq:hd * dq + HEAD_DIM] = qh
            qaug_ref[:, hd * dq + HEAD_DIM:(hd + 1) * dq] = logits.astype(BF16)
        q_src = qaug_ref
    else:
        q_src = q_ref

    def tile(t):
        return pl.ds(pl.multiple_of(t * ATT_TK, ATT_TK), ATT_TK)

    r = lax.broadcasted_iota(jnp.int32, (ATT_TQ, ATT_TQ), 0)
    c = lax.broadcasted_iota(jnp.int32, (ATT_TQ, ATT_TQ), 1)
    tri = c <= r
    if ATT_TK == ATT_TQ:
        t_diag = qi
        mask = tri
    else:
        t_diag = qi // 2
        odd = (qi % 2) == 1
        mask = jnp.concatenate([tri | odd, tri & odd], axis=1)
    for hd in range(n_heads):
        s = lax.dot_general(q_src[:, hd * dq:(hd + 1) * dq], k_ref[tile(t_diag), hd * dq:(hd + 1) * dq], nt,
                            preferred_element_type=F32)
        s = jnp.where(mask, s, NEG_INF)
        m = jnp.max(s, axis=-1, keepdims=True)
        p = jnp.exp(s - m)
        m_ref[hd] = m
        l_ref[hd] = jnp.sum(p, axis=-1, keepdims=True)
        acc_ref[hd] = jnp.dot(p.astype(BF16), v_ref[tile(t_diag), hd * dv:(hd + 1) * dv],
                              preferred_element_type=F32)

    def body(t, carry):
        for hd in range(n_heads):
            s = lax.dot_general(q_src[:, hd * dq:(hd + 1) * dq], k_ref[tile(t), hd * dq:(hd + 1) * dq], nt,
                                preferred_element_type=F32)
            m_old = m_ref[hd]
            m_new = jnp.maximum(m_old, jnp.max(s, axis=-1, keepdims=True))
            alpha = jnp.exp(m_old - m_new)
            p = jnp.exp(s - m_new)
            m_ref[hd] = m_new
            l_ref[hd] = alpha * l_ref[hd] + jnp.sum(p, axis=-1, keepdims=True)
            acc_ref[hd] = alpha * acc_ref[hd] + jnp.dot(p.astype(BF16), v_ref[tile(t), hd * dv:(hd + 1) * dv],
                                                        preferred_element_type=F32)
        return carry

    lax.fori_loop(0, t_diag, body, 0)
    for hd in range(n_heads):
        o_ref[:, hd * dv:(hd + 1) * dv] = (acc_ref[hd] / l_ref[hd]).astype(o_ref.dtype)


def _attention(q, k, v, n_heads, dq, dv, moba):
    nq = SEQ // ATT_TQ
    scratch = [pltpu.VMEM((n_heads, ATT_TQ, 1), F32), pltpu.VMEM((n_heads, ATT_TQ, 1), F32),
               pltpu.VMEM((n_heads, ATT_TQ, dv), F32)]
    if moba:
        scratch += [pltpu.VMEM((LANES, n_heads * dq), F32), pltpu.VMEM((ATT_TQ, n_heads * dq), BF16)]
    return pl.pallas_call(
        functools.partial(_attn_kernel, n_heads=n_heads, dq=dq, dv=dv, moba=moba),
        grid=(BATCH, nq),
        in_specs=[
            pl.BlockSpec((ATT_TQ, q.shape[1]), lambda b, i: (b * nq + i, 0)),
            pl.BlockSpec((SEQ, n_heads * dq), lambda b, i: (b, 0)),
            pl.BlockSpec((SEQ, n_heads * dv), lambda b, i: (b, 0)),
        ],
        out_specs=pl.BlockSpec((ATT_TQ, n_heads * dv), lambda b, i: (b * nq + i, 0)),
        out_shape=jax.ShapeDtypeStruct((TOKENS, n_heads * dv), BF16),
        scratch_shapes=scratch,
        compiler_params=_params("parallel", "arbitrary"),
        name="moba_attn" if moba else "mla_attn",
    )(q, k, v)


def _swa_kernel(sink_ref, q_ref, kp_ref, kc_ref, vp_ref, vc_ref, o_ref):
    hk = pl.program_id(1)
    n = pl.program_id(2)
    W = SWA_WINDOW
    nt = (((1,), (1,)), ((), ()))
    kk = jnp.concatenate([kp_ref[...], kc_ref[...]], axis=0)
    vv = jnp.concatenate([vp_ref[...], vc_ref[...]], axis=0)
    r = lax.broadcasted_iota(jnp.int32, (W, 2 * W), 0)
    c = lax.broadcasted_iota(jnp.int32, (W, 2 * W), 1)
    rel = r + W - c
    mask = (rel >= 0) & (rel < W) & ((c >= W) | (n > 0))
    for g in range(SWA_GROUP):
        sink = sink_ref[hk * SWA_GROUP + g]
        s = lax.dot_general(q_ref[:, g * LANES:(g + 1) * LANES], kk, nt, preferred_element_type=F32)
        s = jnp.where(mask, s, NEG_INF)
        m = jnp.maximum(jnp.max(s, axis=-1, keepdims=True), sink)
        p = jnp.exp(s - m)
        denom = jnp.sum(p, axis=-1, keepdims=True) + jnp.exp(sink - m)
        o = jnp.dot(p.astype(BF16), vv, preferred_element_type=F32)
        o_ref[:, g * LANES:(g + 1) * LANES] = (o / denom).astype(o_ref.dtype)


def _swa(sinks, q, k, v):
    nb = SEQ // SWA_WINDOW
    W = SWA_WINDOW
    cur = lambda b, h, n: (b * nb + n, h)
    prev = lambda b, h, n: (b * nb + jnp.maximum(n - 1, 0), h)
    return pl.pallas_call(
        _swa_kernel,
        grid=(BATCH, N_SWA_KV_HEADS, nb),
        in_specs=[
            pl.BlockSpec(memory_space=pltpu.SMEM),
            pl.BlockSpec((W, SWA_GROUP * HEAD_DIM), cur),
            pl.BlockSpec((W, HEAD_DIM), prev),
            pl.BlockSpec((W, HEAD_DIM), cur),
            pl.BlockSpec((W, HEAD_DIM), prev),
            pl.BlockSpec((W, HEAD_DIM), cur),
        ],
        out_specs=pl.BlockSpec((W, SWA_GROUP * HEAD_DIM), cur),
        out_shape=jax.ShapeDtypeStruct((TOKENS, N_SWA_HEADS * HEAD_DIM), BF16),
        compiler_params=_params("parallel", "parallel", "arbitrary"),
        name="swa_attn",
    )(sinks, q, k, k, v, v)


OUT_TM = 512
W_MLA = N_MLA_HEADS * MLA_V_DIM
W_MOBA = N_MOBA_HEADS * HEAD_DIM
W_SWA = N_SWA_HEADS * HEAD_DIM


def _outproj_kernel(x_ref, a_ref, b_ref, c_ref, w_ref, o_ref):
    acc = jnp.dot(a_ref[...], w_ref[0:W_MLA, :], preferred_element_type=F32)
    acc += jnp.dot(b_ref[...], w_ref[W_MLA:W_MLA + W_MOBA, :], preferred_element_type=F32)
    acc += jnp.dot(c_ref[...], w_ref[W_MLA + W_MOBA:, :], preferred_element_type=F32)
    o_ref[...] = x_ref[...] + acc


def _outproj(x, a, b, c, w):
    row = lambda i: (i, 0)
    return pl.pallas_call(
        _outproj_kernel,
        grid=(TOKENS // OUT_TM,),
        in_specs=[
            pl.BlockSpec((OUT_TM, D_MODEL), row),
            pl.BlockSpec((OUT_TM, W_MLA), row),
            pl.BlockSpec((OUT_TM, W_MOBA), row),
            pl.BlockSpec((OUT_TM, W_SWA), row),
            pl.BlockSpec((D_MODEL, D_MODEL), lambda i: (0, 0)),
        ],
        out_specs=pl.BlockSpec((OUT_TM, D_MODEL), row),
        out_shape=jax.ShapeDtypeStruct((TOKENS, D_MODEL), F32),
        compiler_params=_params("parallel"),
        name="out_proj",
    )(x, a, b, c, w)


FIN_TM = 512


def _final_kernel(x_ref, g_ref, o_ref):
    o_ref[...] = _rmsnorm_rows(x_ref[...], g_ref[...])


def _final_norm(x, g):
    row = lambda i: (i, 0)
    return pl.pallas_call(
        _final_kernel,
        grid=(TOKENS // FIN_TM,),
        in_specs=[pl.BlockSpec((FIN_TM, D_MODEL), row), pl.BlockSpec((1, D_MODEL), lambda i: (0, 0))],
        out_specs=pl.BlockSpec((FIN_TM, D_MODEL), row),
        out_shape=jax.ShapeDtypeStruct((TOKENS, D_MODEL), F32),
        compiler_params=_params("parallel"),
        name="final_norm",
    )(x, g.reshape(1, D_MODEL))


def _rope_tables():
    def angles(d):
        half = d // 2
        inv_freq = 1.0 / (ROPE_THETA ** (jnp.arange(half, dtype=F32) * (2.0 / d)))
        return jnp.arange(SEQ, dtype=F32)[:, None] * inv_freq[None, :]

    ang = angles(HEAD_DIM)
    cos = jnp.concatenate([jnp.cos(ang), jnp.cos(ang)], axis=1)
    sin = jnp.concatenate([-jnp.sin(ang), jnp.sin(ang)], axis=1)
    angm = angles(MLA_ROPE_DIM)
    zero = jnp.zeros_like(angm)
    cosm = jnp.concatenate([jnp.cos(angm), zero, jnp.cos(angm), zero], axis=1)
    sinm = jnp.concatenate([-jnp.sin(angm), zero, jnp.sin(angm), zero], axis=1)
    return cos, sin, cosm, sinm


def _spread_rope_cols(w):
    half = MLA_ROPE_DIM // 2
    zero = jnp.zeros(w.shape[:-1] + (half,), w.dtype)
    return jnp.concatenate([w[..., :half], zero, w[..., half:], zero], axis=-1)


def _layout_w_in(w):
    cuts = np.cumsum([MLA_Q_LORA, MLA_KV_LORA, MLA_ROPE_DIM]).tolist()
    cq, ckv, krope, rest = jnp.split(w, cuts, axis=-1)
    pad = jnp.zeros((D_MODEL, Z_WIDTH - Z_SV - N_SWA_KV_HEADS * HEAD_DIM), w.dtype)
    return jnp.concatenate([cq, ckv, _spread_rope_cols(krope), rest, pad], axis=-1).astype(BF16)


def _layout_w_uq(w):
    w = w.reshape(MLA_Q_LORA, N_MLA_HEADS, MLA_NOPE_DIM + MLA_ROPE_DIM)
    w = jnp.concatenate([w[..., :MLA_NOPE_DIM], _spread_rope_cols(w[..., MLA_NOPE_DIM:])], axis=-1)
    return w.reshape(MLA_Q_LORA, N_MLA_HEADS * MLA_QK_PAD).astype(BF16)


def kernel(x, ffn1_norm, ffn1_w_gate, ffn1_w_up, ffn1_w_down, attn_norm, w_in, mla_q_norm, mla_w_uq, mla_kv_norm, mla_w_ukv, swa_sinks, w_out, ffn2_norm, ffn2_w_gate, ffn2_w_up, ffn2_w_down, final_norm):
    tabs = _rope_tables()
    x = x.reshape(TOKENS, D_MODEL)
    ffn1_g = ffn1_norm.reshape(DEPTH, 1, D_MODEL)
    ffn2_g = ffn2_norm.reshape(DEPTH, 1, D_MODEL)
    for l in range(DEPTH):
        x = _ffn(x, ffn1_g, ffn1_w_gate, ffn1_w_up, ffn1_w_down, l)
        z = _proj(x, attn_norm[l], _layout_w_in(w_in[l]))
        (mla_q, mla_k, mla_v, mq, mk, mv, sq, sk, sv) = _prep(
            z, tabs, mla_q_norm[l], mla_kv_norm[l], _layout_w_uq(mla_w_uq[l]), mla_w_ukv[l].astype(BF16))
        o_mla = _attention(mla_q, mla_k, mla_v, N_MLA_HEADS, MLA_QK_PAD, MLA_V_DIM, moba=False)
        o_moba = _attention(mq, mk, mv, N_MOBA_HEADS, MOBA_QK_AUG, HEAD_DIM, moba=True)
        o_swa = _swa(swa_sinks[l], sq, sk, sv)
        x = _outproj(x, o_mla, o_moba, o_swa, w_out[l].astype(BF16))
        x = _ffn(x, ffn2_g, ffn2_w_gate, ffn2_w_up, ffn2_w_down, l)
    return _final_norm(x, final_norm).reshape(BATCH, SEQ, D_MODEL)
```

```python
import functools

import jax
import jax.numpy as jnp
import numpy as np
from jax import lax
from jax.experimental import pallas as pl
from jax.experimental.pallas import tpu as pltpu

D_MODEL = 2048
BATCH = 2
SEQ = 4096
DEPTH = 2
TOKENS = BATCH * SEQ

HEAD_DIM = 128
N_MLA_HEADS = 4
MLA_Q_LORA = 512
MLA_KV_LORA = 256
MLA_NOPE_DIM = 128
MLA_ROPE_DIM = 64
MLA_V_DIM = 128
MLA_QK_PAD = 256
N_MOBA_HEADS = 4
MOBA_BLOCK = 256
MOBA_TOPK = 3
MOBA_QK_AUG = 256
N_SWA_HEADS = 8
N_SWA_KV_HEADS = 2
SWA_GROUP = N_SWA_HEADS // N_SWA_KV_HEADS
SWA_WINDOW = 128
D_FF = 5632
ROPE_THETA = 10000.0
NORM_EPS = 1e-6

LANES = 128
Z_WIDTH = 4096

Z_CQ = 0
Z_CKV = 512
Z_KROPE = 768
Z_MQ = 896
Z_MK = 1408
Z_MV = 1920
Z_SQ = 2432
Z_SK = 3456
Z_SV = 3712

VMEM_LIMIT = 56 * 1024 * 1024

BF16 = jnp.bfloat16
F32 = jnp.float32
NEG_INF = float("-inf")


def _params(*sem, flags=None):
    return pltpu.CompilerParams(dimension_semantics=sem, vmem_limit_bytes=VMEM_LIMIT, flags=flags)


FFN_TM = 1024
FFN_TF = 256
FFN_TAIL_CHUNKS = 2
FFN_DOWN_TN = 512
NORM_ROWS = 128


def _rmsnorm_rows(x, g):
    ms = jnp.mean(x * x, axis=-1, keepdims=True)
    return x * lax.rsqrt(ms + NORM_EPS) * g


def _norm_into(x_ref, g_ref, h_ref, rows, copy_ref=None):
    g = g_ref[...]

    def body(r, carry):
        sl = pl.ds(pl.multiple_of(r * NORM_ROWS, NORM_ROWS), NORM_ROWS)
        x = x_ref[sl, :]
        h_ref[sl, :] = _rmsnorm_rows(x, g).astype(BF16)
        if copy_ref is not None:
            copy_ref[sl, :] = x
        return carry

    lax.fori_loop(0, rows // NORM_ROWS, body, 0)


def _ffn_chunk(h, wg, wu, wd_cols, o_ref):
    gate = jnp.dot(h, wg, preferred_element_type=F32)
    up = jnp.dot(h, wu, preferred_element_type=F32)
    act = (gate * jax.nn.sigmoid(gate) * (0.5 * up)).astype(BF16)
    for c in range(D_MODEL // FFN_DOWN_TN):
        cols = slice(c * FFN_DOWN_TN, (c + 1) * FFN_DOWN_TN)
        o_ref[:, cols] += jnp.dot(act, wd_cols(cols), preferred_element_type=F32)


def _ffn_head_kernel(x_ref, g_ref, wg_ref, wu_ref, wd_ref, o_ref, wg16_ref, wu16_ref, wd16_ref, h_ref):
    @pl.when(pl.program_id(0) == 0)
    def _():
        _norm_into(x_ref, g_ref, h_ref, FFN_TM, copy_ref=o_ref)

    wg16_ref[...] = wg_ref[...].astype(BF16)
    wu16_ref[...] = wu_ref[...].astype(BF16)
    wd16_ref[...] = wd_ref[...].astype(BF16)
    _ffn_chunk(h_ref[...], wg16_ref[...], wu16_ref[...], lambda cols: wd16_ref[:, cols], o_ref)


def _ffn_tail_kernel(head_out_ref, x_ref, g_ref, wg16_ref, wu16_ref, wd16_ref, o_ref, h_ref):
    del head_out_ref

    @pl.when(pl.program_id(1) == 0)
    def _():
        _norm_into(x_ref, g_ref, h_ref, FFN_TM, copy_ref=o_ref)

    for c in range(FFN_TAIL_CHUNKS):
        rows = slice(c * FFN_TF, (c + 1) * FFN_TF)
        _ffn_chunk(h_ref[...], wg16_ref[c], wu16_ref[c],
                   lambda cols, rows=rows: wd16_ref[rows, cols], o_ref)


def _ffn(x, g, wg, wu, wd, layer):
    n_chunks = D_FF // FFN_TF
    out, wg16, wu16, wd16 = pl.pallas_call(
        _ffn_head_kernel,
        grid=(n_chunks,),
        in_specs=[
            pl.BlockSpec((FFN_TM, D_MODEL), lambda j: (0, 0)),
            pl.BlockSpec((None, 1, D_MODEL), lambda j: (layer, 0, 0)),
            pl.BlockSpec((None, D_MODEL, FFN_TF), lambda j: (layer, 0, j)),
            pl.BlockSpec((None, D_MODEL, FFN_TF), lambda j: (layer, 0, j)),
            pl.BlockSpec((None, FFN_TF, D_MODEL), lambda j: (layer, j, 0)),
        ],
        out_specs=[
            pl.BlockSpec((FFN_TM, D_MODEL), lambda j: (0, 0)),
            pl.BlockSpec((None, D_MODEL, FFN_TF), lambda j: (j, 0, 0)),
            pl.BlockSpec((None, D_MODEL, FFN_TF), lambda j: (j, 0, 0)),
            pl.BlockSpec((FFN_TF, D_MODEL), lambda j: (j, 0)),
        ],
        out_shape=[
            jax.ShapeDtypeStruct((TOKENS, D_MODEL), F32),
            jax.ShapeDtypeStruct((n_chunks, D_MODEL, FFN_TF), BF16),
            jax.ShapeDtypeStruct((n_chunks, D_MODEL, FFN_TF), BF16),
            jax.ShapeDtypeStruct((D_FF, D_MODEL), BF16),
        ],
        scratch_shapes=[pltpu.VMEM((FFN_TM, D_MODEL), BF16)],
        compiler_params=_params("arbitrary"),
        name="ffn_head",
    )(x, g, wg, wu, wd)
    return pl.pallas_call(
        _ffn_tail_kernel,
        grid=(TOKENS // FFN_TM - 1, n_chunks // FFN_TAIL_CHUNKS),
        in_specs=[
            pl.BlockSpec(memory_space=pl.ANY),
            pl.BlockSpec((FFN_TM, D_MODEL), lambda i, j: (i + 1, 0)),
            pl.BlockSpec((None, 1, D_MODEL), lambda i, j: (layer, 0, 0)),
            pl.BlockSpec((FFN_TAIL_CHUNKS, D_MODEL, FFN_TF), lambda i, j: (j, 0, 0)),
            pl.BlockSpec((FFN_TAIL_CHUNKS, D_MODEL, FFN_TF), lambda i, j: (j, 0, 0)),
            pl.BlockSpec((FFN_TAIL_CHUNKS * FFN_TF, D_MODEL), lambda i, j: (j, 0)),
        ],
        out_specs=pl.BlockSpec((FFN_TM, D_MODEL), lambda i, j: (i + 1, 0)),
        out_shape=jax.ShapeDtypeStruct((TOKENS, D_MODEL), F32),
        input_output_aliases={0: 0},
        scratch_shapes=[pltpu.VMEM((FFN_TM, D_MODEL), BF16)],
        compiler_params=_params("parallel", "arbitrary"),
        name="ffn_tail",
    )(out, x, g, wg16, wu16, wd16)


PROJ_TN = 512
PREP_TM = 512


def _project_into(x_ref, g_ref, w_ref, z_ref, h_ref):
    _norm_into(x_ref, g_ref, h_ref, PREP_TM)
    for c in range(Z_WIDTH // PROJ_TN):
        cols = slice(c * PROJ_TN, (c + 1) * PROJ_TN)
        z_ref[:, cols] = jnp.dot(h_ref[...], w_ref[:, cols], preferred_element_type=F32)


def _rope(x, cos, sin_signed):
    return x * cos + pltpu.roll(x, LANES // 2, axis=1) * sin_signed


def _prep_kernel(x_ref, g_ref, w_ref, cos_ref, sin_ref, cosm_ref, sinm_ref, qn_ref, kvn_ref,
                 wuq_ref, wukv_ref,
                 mlaq_ref, mlak_ref, mlav_ref, mq_ref, mk_ref, mv_ref,
                 sq_ref, sk_ref, sv_ref, z_ref, h_ref):
    _project_into(x_ref, g_ref, w_ref, z_ref, h_ref)
    cos = cos_ref[...]
    sin = sin_ref[...]
    cosm = cosm_ref[...]
    sinm = sinm_ref[...]
    mla_scale = (MLA_NOPE_DIM + MLA_ROPE_DIM) ** -0.5
    scale = HEAD_DIM ** -0.5

    cq = _rmsnorm_rows(z_ref[:, Z_CQ:Z_CQ + MLA_Q_LORA], qn_ref[...]).astype(BF16)
    q = jnp.dot(cq, wuq_ref[...], preferred_element_type=F32)
    for hd in range(N_MLA_HEADS):
        base = hd * MLA_QK_PAD
        mlaq_ref[:, base:base + LANES] = (q[:, base:base + LANES] * mla_scale).astype(BF16)
        pe = _rope(q[:, base + LANES:base + 2 * LANES], cosm, sinm)
        mlaq_ref[:, base + LANES:base + 2 * LANES] = (pe * mla_scale).astype(BF16)

    ckv = _rmsnorm_rows(z_ref[:, Z_CKV:Z_CKV + MLA_KV_LORA], kvn_ref[...]).astype(BF16)
    kv = jnp.dot(ckv, wukv_ref[...], preferred_element_type=F32)
    kpe = _rope(z_ref[:, Z_KROPE:Z_KROPE + LANES], cosm, sinm).astype(BF16)
    for hd in range(N_MLA_HEADS):
        base = hd * MLA_QK_PAD
        mlak_ref[:, base:base + LANES] = kv[:, base:base + LANES].astype(BF16)
        mlak_ref[:, base + LANES:base + 2 * LANES] = kpe
        mlav_ref[:, hd * LANES:(hd + 1) * LANES] = kv[:, base + LANES:base + 2 * LANES].astype(BF16)

    pos = (pl.program_id(0) % (SEQ // PREP_TM)) * PREP_TM + lax.broadcasted_iota(jnp.int32, (PREP_TM, LANES), 0)
    lane = lax.broadcasted_iota(jnp.int32, (PREP_TM, LANES), 1)
    block_onehot = jnp.where(pos // MOBA_BLOCK == lane, 1.0, 0.0).astype(BF16)
    for hd in range(N_MOBA_HEADS):
        c = hd * LANES
        mq_ref[:, c:c + LANES] = (_rope(z_ref[:, Z_MQ + c:Z_MQ + c + LANES], cos, sin) * scale).astype(BF16)
        mk_ref[:, 2 * c:2 * c + LANES] = _rope(z_ref[:, Z_MK + c:Z_MK + c + LANES], cos, sin).astype(BF16)
        mk_ref[:, 2 * c + LANES:2 * c + 2 * LANES] = block_onehot
    mv_ref[...] = z_ref[:, Z_MV:Z_MV + N_MOBA_HEADS * LANES].astype(BF16)

    for hd in range(N_SWA_HEADS):
        c = hd * LANES
        sq_ref[:, c:c + LANES] = (_rope(z_ref[:, Z_SQ + c:Z_SQ + c + LANES], cos, sin) * scale).astype(BF16)
    for hd in range(N_SWA_KV_HEADS):
        c = hd * LANES
        sk_ref[:, c:c + LANES] = _rope(z_ref[:, Z_SK + c:Z_SK + c + LANES], cos, sin).astype(BF16)
    sv_ref[...] = z_ref[:, Z_SV:Z_SV + N_SWA_KV_HEADS * LANES].astype(BF16)


def _prep(x, g, w_in, tabs, qn, kvn, wuq, wukv):
    nblk = SEQ // PREP_TM
    row = lambda i: (i, 0)
    tab = lambda i: (i % nblk, 0)
    const = lambda i: (0, 0)
    widths = [N_MLA_HEADS * MLA_QK_PAD, N_MLA_HEADS * MLA_QK_PAD, N_MLA_HEADS * MLA_V_DIM,
              N_MOBA_HEADS * HEAD_DIM, N_MOBA_HEADS * MOBA_QK_AUG, N_MOBA_HEADS * HEAD_DIM,
              N_SWA_HEADS * HEAD_DIM, N_SWA_KV_HEADS * HEAD_DIM, N_SWA_KV_HEADS * HEAD_DIM]
    return pl.pallas_call(
        _prep_kernel,
        grid=(TOKENS // PREP_TM,),
        in_specs=[
            pl.BlockSpec((PREP_TM, D_MODEL), row),
            pl.BlockSpec((1, D_MODEL), const),
            pl.BlockSpec((D_MODEL, Z_WIDTH), const, pipeline_mode=pl.Buffered(1)),
            pl.BlockSpec((PREP_TM, LANES), tab),
            pl.BlockSpec((PREP_TM, LANES), tab),
            pl.BlockSpec((PREP_TM, LANES), tab),
            pl.BlockSpec((PREP_TM, LANES), tab),
            pl.BlockSpec((1, MLA_Q_LORA), const),
            pl.BlockSpec((1, MLA_KV_LORA), const),
            pl.BlockSpec((MLA_Q_LORA, N_MLA_HEADS * MLA_QK_PAD), const),
            pl.BlockSpec((MLA_KV_LORA, N_MLA_HEADS * MLA_QK_PAD), const),
        ],
        out_specs=[pl.BlockSpec((PREP_TM, w), row) for w in widths],
        out_shape=[jax.ShapeDtypeStruct((TOKENS, w), BF16) for w in widths],
        scratch_shapes=[pltpu.VMEM((PREP_TM, Z_WIDTH), F32), pltpu.VMEM((PREP_TM, D_MODEL), BF16)],
        compiler_params=_params("parallel"),
        name="mixer_prep",
    )(x, g.reshape(1, D_MODEL), w_in, *tabs, qn.reshape(1, -1), kvn.reshape(1, -1), wuq, wukv)


ATT_TQ = MOBA_BLOCK
ATT_TK = 2 * MOBA_BLOCK


def _split_bf16(x):
    hi = x.astype(BF16)
    lo = (x - hi.astype(F32)).astype(BF16)
    return hi, lo


MASKED = -1e30


def _moba_gate_logits(q, kbar, qi):
    nblk = SEQ // MOBA_BLOCK
    nt = (((1,), (1,)), ((), ()))
    kb_hi, kb_lo = _split_bf16(kbar)
    gate = (lax.dot_general(kb_hi, q, nt, preferred_element_type=F32)
            + lax.dot_general(kb_lo, q, nt, preferred_element_type=F32))[:nblk]
    blk = lax.broadcasted_iota(jnp.int32, gate.shape, 0)
    gate = jnp.where(blk < qi, gate, NEG_INF)
    rank = jnp.zeros(gate.shape, jnp.int32)
    for other in range(nblk):
        row = gate[other:other + 1, :]
        beats = (row > gate) | ((row == gate) & (other < blk))
        rank = rank + jnp.where(beats, 1, 0)
    keep = ((rank < MOBA_TOPK) & (blk < qi)) | (blk == qi)
    logit_t = jnp.where(keep, 0.0, MASKED)
    logit_t = jnp.concatenate([logit_t, jnp.full((LANES - nblk, gate.shape[1]), MASKED, F32)], axis=0)
    return logit_t.T


def _attn_kernel(q_ref, k_ref, v_ref, o_ref, m_ref, l_ref, acc_ref, *moba_scratch, n_heads, dq, dv, moba):
    qi = pl.program_id(1)
    nt = (((1,), (1,)), ((), ()))

    if moba:
        kbar_ref, qaug_ref = moba_scratch

        @pl.when(qi == 0)
        def _():
            kbar_ref[...] = jnp.zeros_like(kbar_ref)

            def body(b, carry):
                sl = pl.ds(pl.multiple_of(b * MOBA_BLOCK, MOBA_BLOCK), MOBA_BLOCK)
                kbar_ref[pl.ds(b, 1), :] = jnp.mean(k_ref[sl, :].astype(F32), axis=0, keepdims=True)
                return carry

            lax.fori_loop(0, SEQ // MOBA_BLOCK, body, 0)

        for hd in range(n_heads):
            qh = q_ref[:, hd * HEAD_DIM:(hd + 1) * HEAD_DIM]
            logits = _moba_gate_logits(qh, kbar_ref[:, hd * dq:hd * dq + HEAD_DIM], qi)
            qaug_ref[:, hd * dq:hd * dq + HEAD_DIM] = qh
            qaug_ref[:, hd * dq + HEAD_DIM:(hd + 1) * dq] = logits.astype(BF16)
        q_src = qaug_ref
    else:
        q_src = q_ref

    def tile(t):
        return pl.ds(pl.multiple_of(t * ATT_TK, ATT_TK), ATT_TK)

    r = lax.broadcasted_iota(jnp.int32, (ATT_TQ, ATT_TQ), 0)
    c = lax.broadcasted_iota(jnp.int32, (ATT_TQ, ATT_TQ), 1)
    tri = c <= r
    if ATT_TK == ATT_TQ:
        t_diag = qi
        mask = tri
    else:
        t_diag = qi // 2
        odd = (qi % 2) == 1
        mask = jnp.concatenate([tri | odd, tri & odd], axis=1)
    for hd in range(n_heads):
        s = lax.dot_general(q_src[:, hd * dq:(hd + 1) * dq], k_ref[tile(t_diag), hd * dq:(hd + 1) * dq], nt,
                            preferred_element_type=F32)
        s = jnp.where(mask, s, NEG_INF)
        m = jnp.max(s, axis=-1, keepdims=True)
        p = jnp.exp(s - m)
        m_ref[hd] = m
        l_ref[hd] = jnp.sum(p, axis=-1, keepdims=True)
        acc_ref[hd] = jnp.dot(p.astype(BF16), v_ref[tile(t_diag), hd * dv:(hd + 1) * dv],
                              preferred_element_type=F32)

    def body(t, carry):
        for hd in range(n_heads):
            s = lax.dot_general(q_src[:, hd * dq:(hd + 1) * dq], k_ref[tile(t), hd * dq:(hd + 1) * dq], nt,
                                preferred_element_type=F32)
            m_old = m_ref[hd]
            m_new = jnp.maximum(m_old, jnp.max(s, axis=-1, keepdims=True))
            alpha = jnp.exp(m_old - m_new)
            p = jnp.exp(s - m_new)
            m_ref[hd] = m_new
            l_ref[hd] = alpha * l_ref[hd] + jnp.sum(p, axis=-1, keepdims=True)
            acc_ref[hd] = alpha * acc_ref[hd] + jnp.dot(p.astype(BF16), v_ref[tile(t), hd * dv:(hd + 1) * dv],
                                                        preferred_element_type=F32)
        return carry

    lax.fori_loop(0, t_diag, body, 0)
    for hd in range(n_heads):
        o_ref[:, hd * dv:(hd + 1) * dv] = (acc_ref[hd] / l_ref[hd]).astype(o_ref.dtype)


def _attention(q, k, v, n_heads, dq, dv, moba):
    nq = SEQ // ATT_TQ
    scratch = [pltpu.VMEM((n_heads, ATT_TQ, 1), F32), pltpu.VMEM((n_heads, ATT_TQ, 1), F32),
               pltpu.VMEM((n_heads, ATT_TQ, dv), F32)]
    if moba:
        scratch += [pltpu.VMEM((LANES, n_heads * dq), F32), pltpu.VMEM((ATT_TQ, n_heads * dq), BF16)]
    return pl.pallas_call(
        functools.partial(_attn_kernel, n_heads=n_heads, dq=dq, dv=dv, moba=moba),
        grid=(BATCH, nq),
        in_specs=[
            pl.BlockSpec((ATT_TQ, q.shape[1]), lambda b, i: (b * nq + i, 0)),
            pl.BlockSpec((SEQ, n_heads * dq), lambda b, i: (b, 0)),
            pl.BlockSpec((SEQ, n_heads * dv), lambda b, i: (b, 0)),
        ],
        out_specs=pl.BlockSpec((ATT_TQ, n_heads * dv), lambda b, i: (b * nq + i, 0)),
        out_shape=jax.ShapeDtypeStruct((TOKENS, n_heads * dv), BF16),
        scratch_shapes=scratch,
        compiler_params=_params("parallel", "arbitrary"),
        name="moba_attn" if moba else "mla_attn",
    )(q, k, v)


SWA_STEP_BLOCKS = 4


def _swa_kernel(sink_ref, q_ref, kp_ref, kc_ref, vp_ref, vc_ref, o_ref):
    hk = pl.program_id(1)
    n = pl.program_id(2)
    W = SWA_WINDOW
    G = SWA_GROUP
    nt = (((1,), (1,)), ((), ()))
    kk = jnp.concatenate([kp_ref[...], kc_ref[...]], axis=0)
    vv = jnp.concatenate([vp_ref[...], vc_ref[...]], axis=0)
    r = lax.broadcasted_iota(jnp.int32, (G * W, 2 * W), 0) % W
    c = lax.broadcasted_iota(jnp.int32, (G * W, 2 * W), 1)
    rel = r + W - c
    band = (rel >= 0) & (rel < W)
    sink = jnp.concatenate([jnp.full((W, 1), sink_ref[hk * G + g], F32) for g in range(G)], axis=0)
    for blk in range(SWA_STEP_BLOCKS):
        rows = slice(blk * W, (blk + 1) * W)
        q = jnp.concatenate([q_ref[rows, g * LANES:(g + 1) * LANES] for g in range(G)], axis=0)
        s = lax.dot_general(q, kk[blk * W:(blk + 2) * W], nt, preferred_element_type=F32)
        mask = band & ((c >= W) | (n > 0)) if blk == 0 else band
        s = jnp.where(mask, s, NEG_INF)
        m = jnp.maximum(jnp.max(s, axis=-1, keepdims=True), sink)
        p = jnp.exp(s - m)
        denom = jnp.sum(p, axis=-1, keepdims=True) + jnp.exp(sink - m)
        o = jnp.dot(p.astype(BF16), vv[blk * W:(blk + 2) * W], preferred_element_type=F32) / denom
        for g in range(G):
            o_ref[rows, g * LANES:(g + 1) * LANES] = o[g * W:(g + 1) * W].astype(o_ref.dtype)


def _swa(sinks, q, k, v):
    W = SWA_WINDOW
    step_rows = SWA_STEP_BLOCKS * W
    nsteps = SEQ // step_rows
    cur = lambda b, h, n: (b * nsteps + n, h)
    prev = lambda b, h, n: (b * (SEQ // W) + jnp.maximum(n * SWA_STEP_BLOCKS - 1, 0), h)
    return pl.pallas_call(
        _swa_kernel,
        grid=(BATCH, N_SWA_KV_HEADS, nsteps),
        in_specs=[
            pl.BlockSpec(memory_space=pltpu.SMEM),
            pl.BlockSpec((step_rows, SWA_GROUP * HEAD_DIM), cur),
            pl.BlockSpec((W, HEAD_DIM), prev),
            pl.BlockSpec((step_rows, HEAD_DIM), cur),
            pl.BlockSpec((W, HEAD_DIM), prev),
            pl.BlockSpec((step_rows, HEAD_DIM), cur),
        ],
        out_specs=pl.BlockSpec((step_rows, SWA_GROUP * HEAD_DIM), cur),
        out_shape=jax.ShapeDtypeStruct((TOKENS, N_SWA_HEADS * HEAD_DIM), BF16),
        compiler_params=_params("parallel", "parallel", "arbitrary"),
        name="swa_attn",
    )(sinks, q, k, k, v, v)


OUT_TM = 512
W_MLA = N_MLA_HEADS * MLA_V_DIM
W_MOBA = N_MOBA_HEADS * HEAD_DIM
W_SWA = N_SWA_HEADS * HEAD_DIM


def _outproj_kernel(x_ref, a_ref, b_ref, c_ref, w_ref, o_ref):
    acc = jnp.dot(a_ref[...], w_ref[0:W_MLA, :], preferred_element_type=F32)
    acc += jnp.dot(b_ref[...], w_ref[W_MLA:W_MLA + W_MOBA, :], preferred_element_type=F32)
    acc += jnp.dot(c_ref[...], w_ref[W_MLA + W_MOBA:, :], preferred_element_type=F32)
    o_ref[...] = x_ref[...] + acc


def _outproj(x, a, b, c, w):
    row = lambda i: (i, 0)
    return pl.pallas_call(
        _outproj_kernel,
        grid=(TOKENS // OUT_TM,),
        in_specs=[
            pl.BlockSpec((OUT_TM, D_MODEL), row),
            pl.BlockSpec((OUT_TM, W_MLA), row),
            pl.BlockSpec((OUT_TM, W_MOBA), row),
            pl.BlockSpec((OUT_TM, W_SWA), row),
            pl.BlockSpec((D_MODEL, D_MODEL), lambda i: (0, 0)),
        ],
        out_specs=pl.BlockSpec((OUT_TM, D_MODEL), row),
        out_shape=jax.ShapeDtypeStruct((TOKENS, D_MODEL), F32),
        compiler_params=_params("parallel"),
        name="out_proj",
    )(x, a, b, c, w)


FIN_TM = 512


def _final_kernel(x_ref, g_ref, o_ref):
    o_ref[...] = _rmsnorm_rows(x_ref[...], g_ref[...])


def _final_norm(x, g):
    row = lambda i: (i, 0)
    return pl.pallas_call(
        _final_kernel,
        grid=(TOKENS // FIN_TM,),
        in_specs=[pl.BlockSpec((FIN_TM, D_MODEL), row), pl.BlockSpec((1, D_MODEL), lambda i: (0, 0))],
        out_specs=pl.BlockSpec((FIN_TM, D_MODEL), row),
        out_shape=jax.ShapeDtypeStruct((TOKENS, D_MODEL), F32),
        compiler_params=_params("parallel"),
        name="final_norm",
    )(x, g.reshape(1, D_MODEL))


def _rope_tables():
    def angles(d):
        half = d // 2
        inv_freq = 1.0 / (ROPE_THETA ** (jnp.arange(half, dtype=F32) * (2.0 / d)))
        return jnp.arange(SEQ, dtype=F32)[:, None] * inv_freq[None, :]

    ang = angles(HEAD_DIM)
    cos = jnp.concatenate([jnp.cos(ang), jnp.cos(ang)], axis=1)
    sin = jnp.concatenate([-jnp.sin(ang), jnp.sin(ang)], axis=1)
    angm = angles(MLA_ROPE_DIM)
    zero = jnp.zeros_like(angm)
    cosm = jnp.concatenate([jnp.cos(angm), zero, jnp.cos(angm), zero], axis=1)
    sinm = jnp.concatenate([-jnp.sin(angm), zero, jnp.sin(angm), zero], axis=1)
    return cos, sin, cosm, sinm


def _spread_rope_cols(w):
    half = MLA_ROPE_DIM // 2
    zero = jnp.zeros(w.shape[:-1] + (half,), w.dtype)
    return jnp.concatenate([w[..., :half], zero, w[..., half:], zero], axis=-1)


def _layout_w_in(w):
    cuts = np.cumsum([MLA_Q_LORA, MLA_KV_LORA, MLA_ROPE_DIM]).tolist()
    cq, ckv, krope, rest = jnp.split(w, cuts, axis=-1)
    pad = jnp.zeros((D_MODEL, Z_WIDTH - Z_SV - N_SWA_KV_HEADS * HEAD_DIM), w.dtype)
    return jnp.concatenate([cq, ckv, _spread_rope_cols(krope), rest, pad], axis=-1).astype(BF16)


def _layout_w_uq(w):
    w = w.reshape(MLA_Q_LORA, N_MLA_HEADS, MLA_NOPE_DIM + MLA_ROPE_DIM)
    w = jnp.concatenate([w[..., :MLA_NOPE_DIM], _spread_rope_cols(w[..., MLA_NOPE_DIM:])], axis=-1)
    return w.reshape(MLA_Q_LORA, N_MLA_HEADS * MLA_QK_PAD).astype(BF16)


def kernel(x, ffn1_norm, ffn1_w_gate, ffn1_w_up, ffn1_w_down, attn_norm, w_in, mla_q_norm, mla_w_uq, mla_kv_norm, mla_w_ukv, swa_sinks, w_out, ffn2_norm, ffn2_w_gate, ffn2_w_up, ffn2_w_down, final_norm):
    tabs = _rope_tables()
    x = x.reshape(TOKENS, D_MODEL)
    ffn1_g = ffn1_norm.reshape(DEPTH, 1, D_MODEL)
    ffn2_g = ffn2_norm.reshape(DEPTH, 1, D_MODEL)
    for l in range(DEPTH):
        x = _ffn(x, ffn1_g, ffn1_w_gate, ffn1_w_up, ffn1_w_down, l)
        (mla_q, mla_k, mla_v, mq, mk, mv, sq, sk, sv) = _prep(
            x, attn_norm[l], _layout_w_in(w_in[l]), tabs, mla_q_norm[l], mla_kv_norm[l],
            _layout_w_uq(mla_w_uq[l]), mla_w_ukv[l].astype(BF16))
        o_mla = _attention(mla_q, mla_k, mla_v, N_MLA_HEADS, MLA_QK_PAD, MLA_V_DIM, moba=False)
        o_moba = _attention(mq, mk, mv, N_MOBA_HEADS, MOBA_QK_AUG, HEAD_DIM, moba=True)
        o_swa = _swa(swa_sinks[l], sq, sk, sv)
        x = _outproj(x, o_mla, o_moba, o_swa, w_out[l].astype(BF16))
        x = _ffn(x, ffn2_g, ffn2_w_gate, ffn2_w_up, ffn2_w_down, l)
    return _final_norm(x, final_norm).reshape(BATCH, SEQ, D_MODEL)
```

```python
import functools

import jax
import jax.numpy as jnp
import numpy as np
from jax import lax
from jax.experimental import pallas as pl
from jax.experimental.pallas import tpu as pltpu

D_MODEL = 2048
BATCH = 2
SEQ = 4096
DEPTH = 2
TOKENS = BATCH * SEQ

HEAD_DIM = 128
N_MLA_HEADS = 4
MLA_Q_LORA = 512
MLA_KV_LORA = 256
MLA_NOPE_DIM = 128
MLA_ROPE_DIM = 64
MLA_V_DIM = 128
MLA_QK_PAD = 256
N_MOBA_HEADS = 4
MOBA_BLOCK = 256
MOBA_TOPK = 3
MOBA_QK_AUG = 256
N_SWA_HEADS = 8
N_SWA_KV_HEADS = 2
SWA_GROUP = N_SWA_HEADS // N_SWA_KV_HEADS
SWA_WINDOW = 128
D_FF = 5632
ROPE_THETA = 10000.0
NORM_EPS = 1e-6

LANES = 128
Z_WIDTH = 4096

Z_CQ = 0
Z_CKV = 512
Z_KROPE = 768
Z_MQ = 896
Z_MK = 1408
Z_MV = 1920
Z_SQ = 2432
Z_SK = 3456
Z_SV = 3712

VMEM_LIMIT = 56 * 1024 * 1024

BF16 = jnp.bfloat16
F32 = jnp.float32
NEG_INF = float("-inf")


def _params(*sem, flags=None):
    return pltpu.CompilerParams(dimension_semantics=sem, vmem_limit_bytes=VMEM_LIMIT, flags=flags)


FFN_TM = 1024
FFN_TF = 256
FFN_TAIL_CHUNKS = 2
FFN_DOWN_TN = 512
NORM_ROWS = 128


def _rmsnorm_rows(x, g):
    ms = jnp.mean(x * x, axis=-1, keepdims=True)
    return x * lax.rsqrt(ms + NORM_EPS) * g


def _norm_into(x_ref, g_ref, h_ref, rows, copy_ref=None):
    g = g_ref[...]

    def body(r, carry):
        sl = pl.ds(pl.multiple_of(r * NORM_ROWS, NORM_ROWS), NORM_ROWS)
        x = x_ref[sl, :]
        h_ref[sl, :] = _rmsnorm_rows(x, g).astype(BF16)
        if copy_ref is not None:
            copy_ref[sl, :] = x
        return carry

    lax.fori_loop(0, rows // NORM_ROWS, body, 0)


def _ffn_chunk(h, wg, wu, wd_cols, o_ref):
    gate = jnp.dot(h, wg, preferred_element_type=F32)
    up = jnp.dot(h, wu, preferred_element_type=F32)
    act = (gate * jax.nn.sigmoid(gate) * (0.5 * up)).astype(BF16)
    for c in range(D_MODEL // FFN_DOWN_TN):
        cols = slice(c * FFN_DOWN_TN, (c + 1) * FFN_DOWN_TN)
        o_ref[:, cols] += jnp.dot(act, wd_cols(cols), preferred_element_type=F32)


def _norm_in_place(o_ref, g_ref, rows):
    g = g_ref[...]

    def body(r, carry):
        sl = pl.ds(pl.multiple_of(r * NORM_ROWS, NORM_ROWS), NORM_ROWS)
        o_ref[sl, :] = _rmsnorm_rows(o_ref[sl, :], g)
        return carry

    lax.fori_loop(0, rows // NORM_ROWS, body, 0)


def _ffn_head_kernel(x_ref, g_ref, post_g_ref, wg_ref, wu_ref, wd_ref, o_ref, wg16_ref, wu16_ref, wd16_ref, h_ref,
                     *, post_norm):
    @pl.when(pl.program_id(0) == 0)
    def _():
        _norm_into(x_ref, g_ref, h_ref, FFN_TM, copy_ref=o_ref)

    wg16_ref[...] = wg_ref[...].astype(BF16)
    wu16_ref[...] = wu_ref[...].astype(BF16)
    wd16_ref[...] = wd_ref[...].astype(BF16)
    _ffn_chunk(h_ref[...], wg16_ref[...], wu16_ref[...], lambda cols: wd16_ref[:, cols], o_ref)

    if post_norm:
        @pl.when(pl.program_id(0) == pl.num_programs(0) - 1)
        def _():
            _norm_in_place(o_ref, post_g_ref, FFN_TM)


def _ffn_tail_kernel(head_out_ref, x_ref, g_ref, post_g_ref, wg16_ref, wu16_ref, wd16_ref, o_ref, h_ref,
                     *, post_norm):
    del head_out_ref

    @pl.when(pl.program_id(1) == 0)
    def _():
        _norm_into(x_ref, g_ref, h_ref, FFN_TM, copy_ref=o_ref)

    for c in range(FFN_TAIL_CHUNKS):
        rows = slice(c * FFN_TF, (c + 1) * FFN_TF)
        _ffn_chunk(h_ref[...], wg16_ref[c], wu16_ref[c],
                   lambda cols, rows=rows: wd16_ref[rows, cols], o_ref)

    if post_norm:
        @pl.when(pl.program_id(1) == pl.num_programs(1) - 1)
        def _():
            _norm_in_place(o_ref, post_g_ref, FFN_TM)


def _ffn(x, g, wg, wu, wd, layer, post_g, post_norm=False):
    n_chunks = D_FF // FFN_TF
    out, wg16, wu16, wd16 = pl.pallas_call(
        functools.partial(_ffn_head_kernel, post_norm=post_norm),
        grid=(n_chunks,),
        in_specs=[
            pl.BlockSpec((FFN_TM, D_MODEL), lambda j: (0, 0)),
            pl.BlockSpec((None, 1, D_MODEL), lambda j: (layer, 0, 0)),
            pl.BlockSpec((1, D_MODEL), lambda j: (0, 0)),
            pl.BlockSpec((None, D_MODEL, FFN_TF), lambda j: (layer, 0, j)),
            pl.BlockSpec((None, D_MODEL, FFN_TF), lambda j: (layer, 0, j)),
            pl.BlockSpec((None, FFN_TF, D_MODEL), lambda j: (layer, j, 0)),
        ],
        out_specs=[
            pl.BlockSpec((FFN_TM, D_MODEL), lambda j: (0, 0)),
            pl.BlockSpec((None, D_MODEL, FFN_TF), lambda j: (j, 0, 0)),
            pl.BlockSpec((None, D_MODEL, FFN_TF), lambda j: (j, 0, 0)),
            pl.BlockSpec((FFN_TF, D_MODEL), lambda j: (j, 0)),
        ],
        out_shape=[
            jax.ShapeDtypeStruct((TOKENS, D_MODEL), F32),
            jax.ShapeDtypeStruct((n_chunks, D_MODEL, FFN_TF), BF16),
            jax.ShapeDtypeStruct((n_chunks, D_MODEL, FFN_TF), BF16),
            jax.ShapeDtypeStruct((D_FF, D_MODEL), BF16),
        ],
        scratch_shapes=[pltpu.VMEM((FFN_TM, D_MODEL), BF16)],
        compiler_params=_params("arbitrary"),
        name="ffn_head",
    )(x, g, post_g, wg, wu, wd)
    return pl.pallas_call(
        functools.partial(_ffn_tail_kernel, post_norm=post_norm),
        grid=(TOKENS // FFN_TM - 1, n_chunks // FFN_TAIL_CHUNKS),
        in_specs=[
            pl.BlockSpec(memory_space=pl.ANY),
            pl.BlockSpec((FFN_TM, D_MODEL), lambda i, j: (i + 1, 0)),
            pl.BlockSpec((None, 1, D_MODEL), lambda i, j: (layer, 0, 0)),
            pl.BlockSpec((1, D_MODEL), lambda i, j: (0, 0)),
            pl.BlockSpec((FFN_TAIL_CHUNKS, D_MODEL, FFN_TF), lambda i, j: (j, 0, 0)),
            pl.BlockSpec((FFN_TAIL_CHUNKS, D_MODEL, FFN_TF), lambda i, j: (j, 0, 0)),
            pl.BlockSpec((FFN_TAIL_CHUNKS * FFN_TF, D_MODEL), lambda i, j: (j, 0)),
        ],
        out_specs=pl.BlockSpec((FFN_TM, D_MODEL), lambda i, j: (i + 1, 0)),
        out_shape=jax.ShapeDtypeStruct((TOKENS, D_MODEL), F32),
        input_output_aliases={0: 0},
        scratch_shapes=[pltpu.VMEM((FFN_TM, D_MODEL), BF16)],
        compiler_params=_params("parallel", "arbitrary"),
        name="ffn_tail",
    )(out, x, g, post_g, wg16, wu16, wd16)


PROJ_TN = 512
PREP_TM = 512


def _project_into(x_ref, g_ref, w_ref, z_ref, h_ref):
    _norm_into(x_ref, g_ref, h_ref, PREP_TM)
    for c in range(Z_WIDTH // PROJ_TN):
        cols = slice(c * PROJ_TN, (c + 1) * PROJ_TN)
        z_ref[:, cols] = jnp.dot(h_ref[...], w_ref[:, cols], preferred_element_type=F32)


def _rope(x, cos, sin_signed):
    return x * cos + pltpu.roll(x, LANES // 2, axis=1) * sin_signed


def _prep_kernel(x_ref, g_ref, w_ref, cos_ref, sin_ref, cosm_ref, sinm_ref, qn_ref, kvn_ref,
                 wuq_ref, wukv_ref,
                 mlaq_ref, mlak_ref, mlav_ref, mq_ref, mk_ref, mv_ref,
                 sq_ref, sk_ref, sv_ref, z_ref, h_ref):
    _project_into(x_ref, g_ref, w_ref, z_ref, h_ref)
    cos = cos_ref[...]
    sin = sin_ref[...]
    cosm = cosm_ref[...]
    sinm = sinm_ref[...]
    mla_scale = (MLA_NOPE_DIM + MLA_ROPE_DIM) ** -0.5
    scale = HEAD_DIM ** -0.5

    cq = _rmsnorm_rows(z_ref[:, Z_CQ:Z_CQ + MLA_Q_LORA], qn_ref[...]).astype(BF16)
    q = jnp.dot(cq, wuq_ref[...], preferred_element_type=F32)
    for hd in range(N_MLA_HEADS):
        base = hd * MLA_QK_PAD
        mlaq_ref[:, base:base + LANES] = (q[:, base:base + LANES] * mla_scale).astype(BF16)
        pe = _rope(q[:, base + LANES:base + 2 * LANES], cosm, sinm)
        mlaq_ref[:, base + LANES:base + 2 * LANES] = (pe * mla_scale).astype(BF16)

    ones = jnp.ones((PREP_TM, LANES), BF16)
    ckv = _rmsnorm_rows(z_ref[:, Z_CKV:Z_CKV + MLA_KV_LORA], kvn_ref[...]).astype(BF16)
    kv = jnp.dot(ckv, wukv_ref[...], preferred_element_type=F32)
    kpe = _rope(z_ref[:, Z_KROPE:Z_KROPE + LANES], cosm, sinm).astype(BF16)
    for hd in range(N_MLA_HEADS):
        base = hd * MLA_QK_PAD
        mlak_ref[:, base:base + LANES] = kv[:, base:base + LANES].astype(BF16)
        mlak_ref[:, base + LANES:base + 2 * LANES] = kpe
        mlav_ref[:, base:base + LANES] = kv[:, base + LANES:base + 2 * LANES].astype(BF16)
        mlav_ref[:, base + LANES:base + 2 * LANES] = ones

    pos = (pl.program_id(0) % (SEQ // PREP_TM)) * PREP_TM + lax.broadcasted_iota(jnp.int32, (PREP_TM, LANES), 0)
    lane = lax.broadcasted_iota(jnp.int32, (PREP_TM, LANES), 1)
    block_onehot = jnp.where(pos // MOBA_BLOCK == lane, 1.0, 0.0).astype(BF16)
    for hd in range(N_MOBA_HEADS):
        c = hd * LANES
        mq_ref[:, c:c + LANES] = (_rope(z_ref[:, Z_MQ + c:Z_MQ + c + LANES], cos, sin) * scale).astype(BF16)
        mk_ref[:, 2 * c:2 * c + LANES] = _rope(z_ref[:, Z_MK + c:Z_MK + c + LANES], cos, sin).astype(BF16)
        mk_ref[:, 2 * c + LANES:2 * c + 2 * LANES] = block_onehot
        mv_ref[:, 2 * c:2 * c + LANES] = z_ref[:, Z_MV + c:Z_MV + c + LANES].astype(BF16)
        mv_ref[:, 2 * c + LANES:2 * c + 2 * LANES] = ones

    for hd in range(N_SWA_HEADS):
        c = hd * LANES
        sq_ref[:, c:c + LANES] = (_rope(z_ref[:, Z_SQ + c:Z_SQ + c + LANES], cos, sin) * scale).astype(BF16)
    for hd in range(N_SWA_KV_HEADS):
        c = hd * LANES
        sk_ref[:, c:c + LANES] = _rope(z_ref[:, Z_SK + c:Z_SK + c + LANES], cos, sin).astype(BF16)
    sv_ref[...] = z_ref[:, Z_SV:Z_SV + N_SWA_KV_HEADS * LANES].astype(BF16)


def _prep(x, g, w_in, tabs, qn, kvn, wuq, wukv, layer):
    nblk = SEQ // PREP_TM
    row = lambda i: (i, 0)
    tab = lambda i: (i % nblk, 0)
    const = lambda i: (0, 0)
    of_layer = lambda i: (layer, 0, 0)
    widths = [N_MLA_HEADS * MLA_QK_PAD, N_MLA_HEADS * MLA_QK_PAD, N_MLA_HEADS * 2 * MLA_V_DIM,
              N_MOBA_HEADS * HEAD_DIM, N_MOBA_HEADS * MOBA_QK_AUG, N_MOBA_HEADS * 2 * HEAD_DIM,
              N_SWA_HEADS * HEAD_DIM, N_SWA_KV_HEADS * HEAD_DIM, N_SWA_KV_HEADS * HEAD_DIM]
    return pl.pallas_call(
        _prep_kernel,
        grid=(TOKENS // PREP_TM,),
        in_specs=[
            pl.BlockSpec((PREP_TM, D_MODEL), row),
            pl.BlockSpec((1, D_MODEL), const),
            pl.BlockSpec((None, D_MODEL, Z_WIDTH), of_layer, pipeline_mode=pl.Buffered(1)),
            pl.BlockSpec((PREP_TM, LANES), tab),
            pl.BlockSpec((PREP_TM, LANES), tab),
            pl.BlockSpec((PREP_TM, LANES), tab),
            pl.BlockSpec((PREP_TM, LANES), tab),
            pl.BlockSpec((1, MLA_Q_LORA), const),
            pl.BlockSpec((1, MLA_KV_LORA), const),
            pl.BlockSpec((None, MLA_Q_LORA, N_MLA_HEADS * MLA_QK_PAD), of_layer),
            pl.BlockSpec((None, MLA_KV_LORA, N_MLA_HEADS * MLA_QK_PAD), of_layer),
        ],
        out_specs=[pl.BlockSpec((PREP_TM, w), row) for w in widths],
        out_shape=[jax.ShapeDtypeStruct((TOKENS, w), BF16) for w in widths],
        scratch_shapes=[pltpu.VMEM((PREP_TM, Z_WIDTH), F32), pltpu.VMEM((PREP_TM, D_MODEL), BF16)],
        compiler_params=_params("parallel"),
        name="mixer_prep",
    )(x, g.reshape(1, D_MODEL), w_in, *tabs, qn.reshape(1, -1), kvn.reshape(1, -1), wuq, wukv)


ATT_TQ = MOBA_BLOCK
ATT_TK = 2 * MOBA_BLOCK


def _split_bf16(x):
    hi = x.astype(BF16)
    lo = (x - hi.astype(F32)).astype(BF16)
    return hi, lo


MASKED = -1e30


def _moba_gate_logits(q, kbar, qi):
    nblk = SEQ // MOBA_BLOCK
    nt = (((1,), (1,)), ((), ()))
    kb_hi, kb_lo = _split_bf16(kbar)
    gate = (lax.dot_general(kb_hi, q, nt, preferred_element_type=F32)
            + lax.dot_general(kb_lo, q, nt, preferred_element_type=F32))[:nblk]
    blk = lax.broadcasted_iota(jnp.int32, gate.shape, 0)
    gate = jnp.where(blk < qi, gate, NEG_INF)
    rank = jnp.zeros(gate.shape, jnp.int32)
    for other in range(nblk):
        row = gate[other:other + 1, :]
        beats = (row > gate) | ((row == gate) & (other < blk))
        rank = rank + jnp.where(beats, 1, 0)
    keep = ((rank < MOBA_TOPK) & (blk < qi)) | (blk == qi)
    logit_t = jnp.where(keep, 0.0, MASKED)
    logit_t = jnp.concatenate([logit_t, jnp.full((LANES - nblk, gate.shape[1]), MASKED, F32)], axis=0)
    return logit_t.T


def _attn_kernel(q_ref, k_ref, v_ref, o_ref, m_ref, mnext_ref, acc_ref, s_ref, *moba_scratch,
                 n_heads, dq, dv, moba):
    qi = pl.program_id(1)
    nt = (((1,), (1,)), ((), ()))

    if moba:
        kbar_ref, qaug_ref = moba_scratch

        @pl.when(qi == 0)
        def _():
            kbar_ref[...] = jnp.zeros_like(kbar_ref)

            def body(b, carry):
                sl = pl.ds(pl.multiple_of(b * MOBA_BLOCK, MOBA_BLOCK), MOBA_BLOCK)
                kbar_ref[pl.ds(b, 1), :] = jnp.mean(k_ref[sl, :].astype(F32), axis=0, keepdims=True)
                return carry

            lax.fori_loop(0, SEQ // MOBA_BLOCK, body, 0)

        for hd in range(n_heads):
            qh = q_ref[:, hd * HEAD_DIM:(hd + 1) * HEAD_DIM]
            logits = _moba_gate_logits(qh, kbar_ref[:, hd * dq:hd * dq + HEAD_DIM], qi)
            qaug_ref[:, hd * dq:hd * dq + HEAD_DIM] = qh
            qaug_ref[:, hd * dq + HEAD_DIM:(hd + 1) * dq] = logits.astype(BF16)
        q_src = qaug_ref
    else:
        q_src = q_ref

    def tile(t):
        return pl.ds(pl.multiple_of(t * ATT_TK, ATT_TK), ATT_TK)

    r = lax.broadcasted_iota(jnp.int32, (ATT_TQ, ATT_TQ), 0)
    c = lax.broadcasted_iota(jnp.int32, (ATT_TQ, ATT_TQ), 1)
    tri = c <= r
    if ATT_TK == ATT_TQ:
        t_diag = qi
        mask = tri
    else:
        t_diag = qi // 2
        odd = (qi % 2) == 1
        mask = jnp.concatenate([tri | odd, tri & odd], axis=1)
    dva = 2 * dv

    def scores(t, hd):
        return lax.dot_general(q_src[:, hd * dq:(hd + 1) * dq], k_ref[tile(t), hd * dq:(hd + 1) * dq], nt,
                               preferred_element_type=F32)

    def weighted_values(p, t, hd):
        return jnp.dot(p.astype(BF16), v_ref[tile(t), hd * dva:(hd + 1) * dva], preferred_element_type=F32)

    heads = range(n_heads)
    for hd in heads:
        s = jnp.where(mask, scores(t_diag, hd), NEG_INF)
        m = jnp.max(s, axis=-1, keepdims=True)
        m_ref[hd] = m
        acc_ref[hd] = weighted_values(jnp.exp(s - m), t_diag, hd)

    def body(t, carry):
        t_next = jnp.minimum(t + 1, t_diag - 1)
        for hd in heads:
            m_cur = mnext_ref[hd]
            p = jnp.exp(s_ref[hd] - m_cur)
            s = scores(t_next, hd)
            s_ref[hd] = s
            mnext_ref[hd] = jnp.maximum(m_cur, jnp.max(s, axis=-1, keepdims=True))
            alpha = jnp.exp(m_ref[hd] - m_cur)
            acc_ref[hd] = alpha * acc_ref[hd] + weighted_values(p, t, hd)
            m_ref[hd] = m_cur
        return carry

    for hd in heads:
        s = scores(0, hd)
        s_ref[hd] = s
        mnext_ref[hd] = jnp.maximum(m_ref[hd], jnp.max(s, axis=-1, keepdims=True))
    lax.fori_loop(0, t_diag, body, 0)
    for hd in heads:
        acc = acc_ref[hd]
        o_ref[:, hd * dv:(hd + 1) * dv] = (acc[:, :dv] / acc[:, dv:]).astype(o_ref.dtype)


def _attention(q, k, v, n_heads, dq, dv, moba):
    nq = SEQ // ATT_TQ
    scratch = [pltpu.VMEM((n_heads, ATT_TQ, 1), F32), pltpu.VMEM((n_heads, ATT_TQ, 1), F32),
               pltpu.VMEM((n_heads, ATT_TQ, 2 * dv), F32), pltpu.VMEM((n_heads, ATT_TQ, ATT_TK), F32)]
    if moba:
        scratch += [pltpu.VMEM((LANES, n_heads * dq), F32), pltpu.VMEM((ATT_TQ, n_heads * dq), BF16)]
    return pl.pallas_call(
        functools.partial(_attn_kernel, n_heads=n_heads, dq=dq, dv=dv, moba=moba),
        grid=(BATCH, nq),
        in_specs=[
            pl.BlockSpec((ATT_TQ, q.shape[1]), lambda b, i: (b * nq + i, 0)),
            pl.BlockSpec((SEQ, n_heads * dq), lambda b, i: (b, 0)),
            pl.BlockSpec((SEQ, n_heads * 2 * dv), lambda b, i: (b, 0)),
        ],
        out_specs=pl.BlockSpec((ATT_TQ, n_heads * dv), lambda b, i: (b * nq + i, 0)),
        out_shape=jax.ShapeDtypeStruct((TOKENS, n_heads * dv), BF16),
        scratch_shapes=scratch,
        compiler_params=_params("parallel", "arbitrary"),
        name="moba_attn" if moba else "mla_attn",
    )(q, k, v)


SWA_STEP_BLOCKS = 4


def _swa_kernel(sink_ref, q_ref, kp_ref, kc_ref, vp_ref, vc_ref, o_ref):
    hk = pl.program_id(1)
    n = pl.program_id(2)
    W = SWA_WINDOW
    G = SWA_GROUP
    nt = (((1,), (1,)), ((), ()))
    kk = jnp.concatenate([kp_ref[...], kc_ref[...]], axis=0)
    vv = jnp.concatenate([vp_ref[...], vc_ref[...]], axis=0)
    r = lax.broadcasted_iota(jnp.int32, (G * W, 2 * W), 0) % W
    c = lax.broadcasted_iota(jnp.int32, (G * W, 2 * W), 1)
    rel = r + W - c
    band = (rel >= 0) & (rel < W)
    sink = jnp.concatenate([jnp.full((W, 1), sink_ref[hk * G + g], F32) for g in range(G)], axis=0)
    for blk in range(SWA_STEP_BLOCKS):
        rows = slice(blk * W, (blk + 1) * W)
        q = jnp.concatenate([q_ref[rows, g * LANES:(g + 1) * LANES] for g in range(G)], axis=0)
        s = lax.dot_general(q, kk[blk * W:(blk + 2) * W], nt, preferred_element_type=F32)
        mask = band & ((c >= W) | (n > 0)) if blk == 0 else band
        s = jnp.where(mask, s, NEG_INF)
        m = jnp.maximum(jnp.max(s, axis=-1, keepdims=True), sink)
        p = jnp.exp(s - m)
        denom = jnp.sum(p, axis=-1, keepdims=True) + jnp.exp(sink - m)
        o = jnp.dot(p.astype(BF16), vv[blk * W:(blk + 2) * W], preferred_element_type=F32) / denom
        for g in range(G):
            o_ref[rows, g * LANES:(g + 1) * LANES] = o[g * W:(g + 1) * W].astype(o_ref.dtype)


def _swa(sinks, q, k, v):
    W = SWA_WINDOW
    step_rows = SWA_STEP_BLOCKS * W
    nsteps = SEQ // step_rows
    cur = lambda b, h, n: (b * nsteps + n, h)
    prev = lambda b, h, n: (b * (SEQ // W) + jnp.maximum(n * SWA_STEP_BLOCKS - 1, 0), h)
    return pl.pallas_call(
        _swa_kernel,
        grid=(BATCH, N_SWA_KV_HEADS, nsteps),
        in_specs=[
            pl.BlockSpec(memory_space=pltpu.SMEM),
            pl.BlockSpec((step_rows, SWA_GROUP * HEAD_DIM), cur),
            pl.BlockSpec((W, HEAD_DIM), prev),
            pl.BlockSpec((step_rows, HEAD_DIM), cur),
            pl.BlockSpec((W, HEAD_DIM), prev),
            pl.BlockSpec((step_rows, HEAD_DIM), cur),
        ],
        out_specs=pl.BlockSpec((step_rows, SWA_GROUP * HEAD_DIM), cur),
        out_shape=jax.ShapeDtypeStruct((TOKENS, N_SWA_HEADS * HEAD_DIM), BF16),
        compiler_params=_params("parallel", "parallel", "arbitrary"),
        name="swa_attn",
    )(sinks, q, k, k, v, v)


OUT_TM = 512
W_MLA = N_MLA_HEADS * MLA_V_DIM
W_MOBA = N_MOBA_HEADS * HEAD_DIM
W_SWA = N_SWA_HEADS * HEAD_DIM
MIX_WIDTH = W_MLA + W_MOBA + W_SWA


def _outproj_kernel(x_ref, a_ref, b_ref, c_ref, w_ref, o_ref, w16_ref):
    @pl.when(pl.program_id(0) == 0)
    def _():
        def body(r, carry):
            sl = pl.ds(pl.multiple_of(r * NORM_ROWS, NORM_ROWS), NORM_ROWS)
            w16_ref[sl, :] = w_ref[sl, :].astype(BF16)
            return carry

        lax.fori_loop(0, MIX_WIDTH // NORM_ROWS, body, 0)

    acc = jnp.dot(a_ref[...], w16_ref[0:W_MLA, :], preferred_element_type=F32)
    acc += jnp.dot(b_ref[...], w16_ref[W_MLA:W_MLA + W_MOBA, :], preferred_element_type=F32)
    acc += jnp.dot(c_ref[...], w16_ref[W_MLA + W_MOBA:, :], preferred_element_type=F32)
    o_ref[...] = x_ref[...] + acc


def _outproj(x, a, b, c, w, layer):
    row = lambda i: (i, 0)
    return pl.pallas_call(
        _outproj_kernel,
        grid=(TOKENS // OUT_TM,),
        in_specs=[
            pl.BlockSpec((OUT_TM, D_MODEL), row),
            pl.BlockSpec((OUT_TM, W_MLA), row),
            pl.BlockSpec((OUT_TM, W_MOBA), row),
            pl.BlockSpec((OUT_TM, W_SWA), row),
            pl.BlockSpec((None, MIX_WIDTH, D_MODEL), lambda i: (layer, 0, 0), pipeline_mode=pl.Buffered(1)),
        ],
        out_specs=pl.BlockSpec((OUT_TM, D_MODEL), row),
        out_shape=jax.ShapeDtypeStruct((TOKENS, D_MODEL), F32),
        scratch_shapes=[pltpu.VMEM((MIX_WIDTH, D_MODEL), BF16)],
        compiler_params=_params("arbitrary"),
        name="out_proj",
    )(x, a, b, c, w)


def _rope_tables():
    def angles(d):
        half = d // 2
        inv_freq = 1.0 / (ROPE_THETA ** (jnp.arange(half, dtype=F32) * (2.0 / d)))
        return jnp.arange(SEQ, dtype=F32)[:, None] * inv_freq[None, :]

    ang = angles(HEAD_DIM)
    cos = jnp.concatenate([jnp.cos(ang), jnp.cos(ang)], axis=1)
    sin = jnp.concatenate([-jnp.sin(ang), jnp.sin(ang)], axis=1)
    angm = angles(MLA_ROPE_DIM)
    zero = jnp.zeros_like(angm)
    cosm = jnp.concatenate([jnp.cos(angm), zero, jnp.cos(angm), zero], axis=1)
    sinm = jnp.concatenate([-jnp.sin(angm), zero, jnp.sin(angm), zero], axis=1)
    return cos, sin, cosm, sinm


def _spread_rope_cols(w):
    half = MLA_ROPE_DIM // 2
    zero = jnp.zeros(w.shape[:-1] + (half,), w.dtype)
    return jnp.concatenate([w[..., :half], zero, w[..., half:], zero], axis=-1)


def _layout_w_in(w):
    half = MLA_ROPE_DIM // 2
    w = w.astype(BF16)
    gap = jnp.zeros(w.shape[:-1] + (half,), BF16)
    pad = jnp.zeros(w.shape[:-1] + (Z_WIDTH - Z_SV - N_SWA_KV_HEADS * HEAD_DIM,), BF16)
    return jnp.concatenate([w[..., :Z_KROPE + half], gap, w[..., Z_KROPE + half:Z_KROPE + 2 * half], gap,
                            w[..., Z_KROPE + 2 * half:], pad], axis=-1)


def _layout_w_uq(w):
    w = w.astype(BF16).reshape(DEPTH, MLA_Q_LORA, N_MLA_HEADS, MLA_NOPE_DIM + MLA_ROPE_DIM)
    w = jnp.concatenate([w[..., :MLA_NOPE_DIM], _spread_rope_cols(w[..., MLA_NOPE_DIM:])], axis=-1)
    return w.reshape(DEPTH, MLA_Q_LORA, N_MLA_HEADS * MLA_QK_PAD)


def kernel(x, ffn1_norm, ffn1_w_gate, ffn1_w_up, ffn1_w_down, attn_norm, w_in, mla_q_norm, mla_w_uq, mla_kv_norm, mla_w_ukv, swa_sinks, w_out, ffn2_norm, ffn2_w_gate, ffn2_w_up, ffn2_w_down, final_norm):
    tabs = _rope_tables()
    x = x.reshape(TOKENS, D_MODEL)
    ffn1_g = ffn1_norm.reshape(DEPTH, 1, D_MODEL)
    ffn2_g = ffn2_norm.reshape(DEPTH, 1, D_MODEL)
    w_in16 = _layout_w_in(w_in)
    w_uq16 = _layout_w_uq(mla_w_uq)
    w_ukv16 = mla_w_ukv.astype(BF16)
    final_g = final_norm.reshape(1, D_MODEL)
    for l in range(DEPTH):
        x = _ffn(x, ffn1_g, ffn1_w_gate, ffn1_w_up, ffn1_w_down, l, final_g)
        (mla_q, mla_k, mla_v, mq, mk, mv, sq, sk, sv) = _prep(
            x, attn_norm[l], w_in16, tabs, mla_q_norm[l], mla_kv_norm[l], w_uq16, w_ukv16, l)
        o_mla = _attention(mla_q, mla_k, mla_v, N_MLA_HEADS, MLA_QK_PAD, MLA_V_DIM, moba=False)
        o_moba = _attention(mq, mk, mv, N_MOBA_HEADS, MOBA_QK_AUG, HEAD_DIM, moba=True)
        o_swa = _swa(swa_sinks[l], sq, sk, sv)
        x = _outproj(x, o_mla, o_moba, o_swa, w_out, l)
        x = _ffn(x, ffn2_g, ffn2_w_gate, ffn2_w_up, ffn2_w_down, l, final_g, post_norm=(l == DEPTH - 1))
    return x.reshape(BATCH, SEQ, D_MODEL)
```

```python
import functools

import jax
import jax.numpy as jnp
import numpy as np
from jax import lax
from jax.experimental import pallas as pl
from jax.experimental.pallas import tpu as pltpu

D_MODEL = 2048
BATCH = 2
SEQ = 4096
DEPTH = 2
TOKENS = BATCH * SEQ

HEAD_DIM = 128
N_MLA_HEADS = 4
MLA_Q_LORA = 512
MLA_KV_LORA = 256
MLA_NOPE_DIM = 128
MLA_ROPE_DIM = 64
MLA_V_DIM = 128
MLA_QK_PAD = 256
N_MOBA_HEADS = 4
MOBA_BLOCK = 256
MOBA_TOPK = 3
MOBA_QK_AUG = 256
N_SWA_HEADS = 8
N_SWA_KV_HEADS = 2
SWA_GROUP = N_SWA_HEADS // N_SWA_KV_HEADS
SWA_WINDOW = 128
D_FF = 5632
ROPE_THETA = 10000.0
NORM_EPS = 1e-6

LANES = 128
Z_WIDTH = 4096

Z_CQ = 0
Z_CKV = 512
Z_KROPE = 768
Z_MQ = 896
Z_MK = 1408
Z_MV = 1920
Z_SQ = 2432
Z_SK = 3456
Z_SV = 3712

VMEM_LIMIT = 56 * 1024 * 1024

BF16 = jnp.bfloat16
F32 = jnp.float32
NEG_INF = float("-inf")


def _params(*sem, flags=None):
    return pltpu.CompilerParams(dimension_semantics=sem, vmem_limit_bytes=VMEM_LIMIT, flags=flags)


FFN_TM = 1024
FFN_TF = 256
FFN_TAIL_CHUNKS = 2
FFN_DOWN_TN = 512
NORM_ROWS = 128


def _rmsnorm_rows(x, g):
    ms = jnp.mean(x * x, axis=-1, keepdims=True)
    return x * lax.rsqrt(ms + NORM_EPS) * g


def _norm_into(x_ref, g_ref, h_ref, rows, copy_ref=None):
    g = g_ref[...]

    def body(r, carry):
        sl = pl.ds(pl.multiple_of(r * NORM_ROWS, NORM_ROWS), NORM_ROWS)
        x = x_ref[sl, :]
        h_ref[sl, :] = _rmsnorm_rows(x, g).astype(BF16)
        if copy_ref is not None:
            copy_ref[sl, :] = x
        return carry

    lax.fori_loop(0, rows // NORM_ROWS, body, 0)


def _ffn_chunk(h, wg, wu, wd_cols, o_ref):
    gate = jnp.dot(h, wg, preferred_element_type=F32)
    up = jnp.dot(h, wu, preferred_element_type=F32)
    act = (gate * jax.nn.sigmoid(gate) * (0.5 * up)).astype(BF16)
    for c in range(D_MODEL // FFN_DOWN_TN):
        cols = slice(c * FFN_DOWN_TN, (c + 1) * FFN_DOWN_TN)
        o_ref[:, cols] += jnp.dot(act, wd_cols(cols), preferred_element_type=F32)


def _norm_in_place(o_ref, g_ref, rows):
    g = g_ref[...]

    def body(r, carry):
        sl = pl.ds(pl.multiple_of(r * NORM_ROWS, NORM_ROWS), NORM_ROWS)
        o_ref[sl, :] = _rmsnorm_rows(o_ref[sl, :], g)
        return carry

    lax.fori_loop(0, rows // NORM_ROWS, body, 0)


def _ffn_head_kernel(x_ref, g_ref, post_g_ref, wg_ref, wu_ref, wd_ref, o_ref, wg16_ref, wu16_ref, wd16_ref, h_ref,
                     *, post_norm):
    @pl.when(pl.program_id(0) == 0)
    def _():
        _norm_into(x_ref, g_ref, h_ref, FFN_TM, copy_ref=o_ref)

    wg16_ref[...] = wg_ref[...].astype(BF16)
    wu16_ref[...] = wu_ref[...].astype(BF16)
    wd16_ref[...] = wd_ref[...].astype(BF16)
    _ffn_chunk(h_ref[...], wg16_ref[...], wu16_ref[...], lambda cols: wd16_ref[:, cols], o_ref)

    if post_norm:
        @pl.when(pl.program_id(0) == pl.num_programs(0) - 1)
        def _():
            _norm_in_place(o_ref, post_g_ref, FFN_TM)


def _ffn_tail_kernel(head_out_ref, x_ref, g_ref, post_g_ref, wg16_ref, wu16_ref, wd16_ref, o_ref, h_ref,
                     *, post_norm):
    del head_out_ref

    @pl.when(pl.program_id(1) == 0)
    def _():
        _norm_into(x_ref, g_ref, h_ref, FFN_TM, copy_ref=o_ref)

    for c in range(FFN_TAIL_CHUNKS):
        rows = slice(c * FFN_TF, (c + 1) * FFN_TF)
        _ffn_chunk(h_ref[...], wg16_ref[c], wu16_ref[c],
                   lambda cols, rows=rows: wd16_ref[rows, cols], o_ref)

    if post_norm:
        @pl.when(pl.program_id(1) == pl.num_programs(1) - 1)
        def _():
            _norm_in_place(o_ref, post_g_ref, FFN_TM)


def _ffn(x, g, wg, wu, wd, layer, post_g, post_norm=False):
    n_chunks = D_FF // FFN_TF
    out, wg16, wu16, wd16 = pl.pallas_call(
        functools.partial(_ffn_head_kernel, post_norm=post_norm),
        grid=(n_chunks,),
        in_specs=[
            pl.BlockSpec((FFN_TM, D_MODEL), lambda j: (0, 0)),
            pl.BlockSpec((None, 1, D_MODEL), lambda j: (layer, 0, 0)),
            pl.BlockSpec((1, D_MODEL), lambda j: (0, 0)),
            pl.BlockSpec((None, D_MODEL, FFN_TF), lambda j: (layer, 0, j)),
            pl.BlockSpec((None, D_MODEL, FFN_TF), lambda j: (layer, 0, j)),
            pl.BlockSpec((None, FFN_TF, D_MODEL), lambda j: (layer, j, 0)),
        ],
        out_specs=[
            pl.BlockSpec((FFN_TM, D_MODEL), lambda j: (0, 0)),
            pl.BlockSpec((None, D_MODEL, FFN_TF), lambda j: (j, 0, 0)),
            pl.BlockSpec((None, D_MODEL, FFN_TF), lambda j: (j, 0, 0)),
            pl.BlockSpec((FFN_TF, D_MODEL), lambda j: (j, 0)),
        ],
        out_shape=[
            jax.ShapeDtypeStruct((TOKENS, D_MODEL), F32),
            jax.ShapeDtypeStruct((n_chunks, D_MODEL, FFN_TF), BF16),
            jax.ShapeDtypeStruct((n_chunks, D_MODEL, FFN_TF), BF16),
            jax.ShapeDtypeStruct((D_FF, D_MODEL), BF16),
        ],
        scratch_shapes=[pltpu.VMEM((FFN_TM, D_MODEL), BF16)],
        compiler_params=_params("arbitrary"),
        name="ffn_head",
    )(x, g, post_g, wg, wu, wd)
    return pl.pallas_call(
        functools.partial(_ffn_tail_kernel, post_norm=post_norm),
        grid=(TOKENS // FFN_TM - 1, n_chunks // FFN_TAIL_CHUNKS),
        in_specs=[
            pl.BlockSpec(memory_space=pl.ANY),
            pl.BlockSpec((FFN_TM, D_MODEL), lambda i, j: (i + 1, 0)),
            pl.BlockSpec((None, 1, D_MODEL), lambda i, j: (layer, 0, 0)),
            pl.BlockSpec((1, D_MODEL), lambda i, j: (0, 0)),
            pl.BlockSpec((FFN_TAIL_CHUNKS, D_MODEL, FFN_TF), lambda i, j: (j, 0, 0)),
            pl.BlockSpec((FFN_TAIL_CHUNKS, D_MODEL, FFN_TF), lambda i, j: (j, 0, 0)),
            pl.BlockSpec((FFN_TAIL_CHUNKS * FFN_TF, D_MODEL), lambda i, j: (j, 0)),
        ],
        out_specs=pl.BlockSpec((FFN_TM, D_MODEL), lambda i, j: (i + 1, 0)),
        out_shape=jax.ShapeDtypeStruct((TOKENS, D_MODEL), F32),
        input_output_aliases={0: 0},
        scratch_shapes=[pltpu.VMEM((FFN_TM, D_MODEL), BF16)],
        compiler_params=_params("parallel", "arbitrary"),
        name="ffn_tail",
    )(out, x, g, post_g, wg16, wu16, wd16)


PROJ_TN = 512
PREP_TM = 512


def _project_into(x_ref, g_ref, w_ref, z_ref, h_ref):
    _norm_into(x_ref, g_ref, h_ref, PREP_TM)
    for c in range(Z_WIDTH // PROJ_TN):
        cols = slice(c * PROJ_TN, (c + 1) * PROJ_TN)
        z_ref[:, cols] = jnp.dot(h_ref[...], w_ref[:, cols], preferred_element_type=F32)


def _rope(x, cos, sin_signed):
    return x * cos + pltpu.roll(x, LANES // 2, axis=1) * sin_signed


def _prep_kernel(x_ref, g_ref, w_ref, cos_ref, sin_ref, cosm_ref, sinm_ref, qn_ref, kvn_ref,
                 wuq_ref, wukv_ref,
                 mlaq_ref, mlak_ref, mlav_ref, mq_ref, mk_ref, mv_ref,
                 sq_ref, sk_ref, sv_ref, z_ref, h_ref):
    _project_into(x_ref, g_ref, w_ref, z_ref, h_ref)
    cos = cos_ref[...]
    sin = sin_ref[...]
    cosm = cosm_ref[...]
    sinm = sinm_ref[...]
    mla_scale = (MLA_NOPE_DIM + MLA_ROPE_DIM) ** -0.5
    scale = HEAD_DIM ** -0.5

    cq = _rmsnorm_rows(z_ref[:, Z_CQ:Z_CQ + MLA_Q_LORA], qn_ref[...]).astype(BF16)
    q = jnp.dot(cq, wuq_ref[...], preferred_element_type=F32)
    for hd in range(N_MLA_HEADS):
        base = hd * MLA_QK_PAD
        mlaq_ref[:, base:base + LANES] = (q[:, base:base + LANES] * mla_scale).astype(BF16)
        pe = _rope(q[:, base + LANES:base + 2 * LANES], cosm, sinm)
        mlaq_ref[:, base + LANES:base + 2 * LANES] = (pe * mla_scale).astype(BF16)

    ones = jnp.ones((PREP_TM, LANES), BF16)
    ckv = _rmsnorm_rows(z_ref[:, Z_CKV:Z_CKV + MLA_KV_LORA], kvn_ref[...]).astype(BF16)
    kv = jnp.dot(ckv, wukv_ref[...], preferred_element_type=F32)
    kpe = _rope(z_ref[:, Z_KROPE:Z_KROPE + LANES], cosm, sinm).astype(BF16)
    for hd in range(N_MLA_HEADS):
        base = hd * MLA_QK_PAD
        mlak_ref[:, base:base + LANES] = kv[:, base:base + LANES].astype(BF16)
        mlak_ref[:, base + LANES:base + 2 * LANES] = kpe
        mlav_ref[:, base:base + LANES] = kv[:, base + LANES:base + 2 * LANES].astype(BF16)
        mlav_ref[:, base + LANES:base + 2 * LANES] = ones

    pos = (pl.program_id(0) % (SEQ // PREP_TM)) * PREP_TM + lax.broadcasted_iota(jnp.int32, (PREP_TM, LANES), 0)
    lane = lax.broadcasted_iota(jnp.int32, (PREP_TM, LANES), 1)
    block_onehot = jnp.where(pos // MOBA_BLOCK == lane, 1.0, 0.0).astype(BF16)
    for hd in range(N_MOBA_HEADS):
        c = hd * LANES
        mq_ref[:, c:c + LANES] = (_rope(z_ref[:, Z_MQ + c:Z_MQ + c + LANES], cos, sin) * scale).astype(BF16)
        mk_ref[:, 2 * c:2 * c + LANES] = _rope(z_ref[:, Z_MK + c:Z_MK + c + LANES], cos, sin).astype(BF16)
        mk_ref[:, 2 * c + LANES:2 * c + 2 * LANES] = block_onehot
        mv_ref[:, 2 * c:2 * c + LANES] = z_ref[:, Z_MV + c:Z_MV + c + LANES].astype(BF16)
        mv_ref[:, 2 * c + LANES:2 * c + 2 * LANES] = ones

    for hd in range(N_SWA_HEADS):
        c = hd * LANES
        sq_ref[:, c:c + LANES] = (_rope(z_ref[:, Z_SQ + c:Z_SQ + c + LANES], cos, sin) * scale).astype(BF16)
    for hd in range(N_SWA_KV_HEADS):
        c = hd * LANES
        sk_ref[:, c:c + LANES] = _rope(z_ref[:, Z_SK + c:Z_SK + c + LANES], cos, sin).astype(BF16)
        sv_ref[:, 2 * c:2 * c + LANES] = z_ref[:, Z_SV + c:Z_SV + c + LANES].astype(BF16)
        sv_ref[:, 2 * c + LANES:2 * c + 2 * LANES] = ones


def _prep(x, g, w_in, tabs, qn, kvn, wuq, wukv, layer):
    nblk = SEQ // PREP_TM
    row = lambda i: (i, 0)
    tab = lambda i: (i % nblk, 0)
    const = lambda i: (0, 0)
    of_layer = lambda i: (layer, 0, 0)
    widths = [N_MLA_HEADS * MLA_QK_PAD, N_MLA_HEADS * MLA_QK_PAD, N_MLA_HEADS * 2 * MLA_V_DIM,
              N_MOBA_HEADS * HEAD_DIM, N_MOBA_HEADS * MOBA_QK_AUG, N_MOBA_HEADS * 2 * HEAD_DIM,
              N_SWA_HEADS * HEAD_DIM, N_SWA_KV_HEADS * HEAD_DIM, N_SWA_KV_HEADS * 2 * HEAD_DIM]
    return pl.pallas_call(
        _prep_kernel,
        grid=(TOKENS // PREP_TM,),
        in_specs=[
            pl.BlockSpec((PREP_TM, D_MODEL), row),
            pl.BlockSpec((1, D_MODEL), const),
            pl.BlockSpec((None, D_MODEL, Z_WIDTH), of_layer, pipeline_mode=pl.Buffered(1)),
            pl.BlockSpec((PREP_TM, LANES), tab),
            pl.BlockSpec((PREP_TM, LANES), tab),
            pl.BlockSpec((PREP_TM, LANES), tab),
            pl.BlockSpec((PREP_TM, LANES), tab),
            pl.BlockSpec((1, MLA_Q_LORA), const),
            pl.BlockSpec((1, MLA_KV_LORA), const),
            pl.BlockSpec((None, MLA_Q_LORA, N_MLA_HEADS * MLA_QK_PAD), of_layer),
            pl.BlockSpec((None, MLA_KV_LORA, N_MLA_HEADS * MLA_QK_PAD), of_layer),
        ],
        out_specs=[pl.BlockSpec((PREP_TM, w), row) for w in widths],
        out_shape=[jax.ShapeDtypeStruct((TOKENS, w), BF16) for w in widths],
        scratch_shapes=[pltpu.VMEM((PREP_TM, Z_WIDTH), F32), pltpu.VMEM((PREP_TM, D_MODEL), BF16)],
        compiler_params=_params("parallel"),
        name="mixer_prep",
    )(x, g.reshape(1, D_MODEL), w_in, *tabs, qn.reshape(1, -1), kvn.reshape(1, -1), wuq, wukv)


ATT_TQ = MOBA_BLOCK
ATT_TK = 2 * MOBA_BLOCK


def _split_bf16(x):
    hi = x.astype(BF16)
    lo = (x - hi.astype(F32)).astype(BF16)
    return hi, lo


MASKED = -1e30


def _moba_gate_logits(q, kbar, qi):
    nblk = SEQ // MOBA_BLOCK
    nt = (((1,), (1,)), ((), ()))
    kb_hi, kb_lo = _split_bf16(kbar)
    gate = (lax.dot_general(kb_hi, q, nt, preferred_element_type=F32)
            + lax.dot_general(kb_lo, q, nt, preferred_element_type=F32))[:nblk]
    blk = lax.broadcasted_iota(jnp.int32, gate.shape, 0)
    gate = jnp.where(blk < qi, gate, NEG_INF)
    rank = jnp.zeros(gate.shape, jnp.int32)
    for other in range(nblk):
        row = gate[other:other + 1, :]
        beats = (row > gate) | ((row == gate) & (other < blk))
        rank = rank + jnp.where(beats, 1, 0)
    keep = ((rank < MOBA_TOPK) & (blk < qi)) | (blk == qi)
    logit_t = jnp.where(keep, 0.0, MASKED)
    logit_t = jnp.concatenate([logit_t, jnp.full((LANES - nblk, gate.shape[1]), MASKED, F32)], axis=0)
    return logit_t.T


def _attn_kernel(q_ref, k_ref, v_ref, o_ref, m_ref, mnext_ref, acc_ref, s_ref, *moba_scratch,
                 n_heads, dq, dv, moba):
    qi = pl.program_id(1)
    nt = (((1,), (1,)), ((), ()))

    if moba:
        kbar_ref, qaug_ref = moba_scratch

        @pl.when(qi == 0)
        def _():
            kbar_ref[...] = jnp.zeros_like(kbar_ref)

            def body(b, carry):
                sl = pl.ds(pl.multiple_of(b * MOBA_BLOCK, MOBA_BLOCK), MOBA_BLOCK)
                kbar_ref[pl.ds(b, 1), :] = jnp.mean(k_ref[sl, :].astype(F32), axis=0, keepdims=True)
                return carry

            lax.fori_loop(0, SEQ // MOBA_BLOCK, body, 0)

        for hd in range(n_heads):
            qh = q_ref[:, hd * HEAD_DIM:(hd + 1) * HEAD_DIM]
            logits = _moba_gate_logits(qh, kbar_ref[:, hd * dq:hd * dq + HEAD_DIM], qi)
            qaug_ref[:, hd * dq:hd * dq + HEAD_DIM] = qh
            qaug_ref[:, hd * dq + HEAD_DIM:(hd + 1) * dq] = logits.astype(BF16)
        q_src = qaug_ref
    else:
        q_src = q_ref

    def tile(t):
        return pl.ds(pl.multiple_of(t * ATT_TK, ATT_TK), ATT_TK)

    r = lax.broadcasted_iota(jnp.int32, (ATT_TQ, ATT_TQ), 0)
    c = lax.broadcasted_iota(jnp.int32, (ATT_TQ, ATT_TQ), 1)
    tri = c <= r
    if ATT_TK == ATT_TQ:
        t_diag = qi
        mask = tri
    else:
        t_diag = qi // 2
        odd = (qi % 2) == 1
        mask = jnp.concatenate([tri | odd, tri & odd], axis=1)
    dva = 2 * dv

    def scores(t, hd):
        return lax.dot_general(q_src[:, hd * dq:(hd + 1) * dq], k_ref[tile(t), hd * dq:(hd + 1) * dq], nt,
                               preferred_element_type=F32)

    def weighted_values(p, t, hd):
        return jnp.dot(p.astype(BF16), v_ref[tile(t), hd * dva:(hd + 1) * dva], preferred_element_type=F32)

    heads = range(n_heads)
    for hd in heads:
        s = jnp.where(mask, scores(t_diag, hd), NEG_INF)
        m = jnp.max(s, axis=-1, keepdims=True)
        m_ref[hd] = m
        acc_ref[hd] = weighted_values(jnp.exp(s - m), t_diag, hd)

    def body(t, carry):
        t_next = jnp.minimum(t + 1, t_diag - 1)
        for hd in heads:
            m_cur = mnext_ref[hd]
            p = jnp.exp(s_ref[hd] - m_cur)
            s = scores(t_next, hd)
            s_ref[hd] = s
            mnext_ref[hd] = jnp.maximum(m_cur, jnp.max(s, axis=-1, keepdims=True))
            alpha = jnp.exp(m_ref[hd] - m_cur)
            acc_ref[hd] = alpha * acc_ref[hd] + weighted_values(p, t, hd)
            m_ref[hd] = m_cur
        return carry

    for hd in heads:
        s = scores(0, hd)
        s_ref[hd] = s
        mnext_ref[hd] = jnp.maximum(m_ref[hd], jnp.max(s, axis=-1, keepdims=True))
    lax.fori_loop(0, t_diag, body, 0)
    for hd in heads:
        acc = acc_ref[hd]
        o_ref[:, hd * dv:(hd + 1) * dv] = (acc[:, :dv] / acc[:, dv:]).astype(o_ref.dtype)


def _attention(q, k, v, n_heads, dq, dv, moba):
    nq = SEQ // ATT_TQ
    scratch = [pltpu.VMEM((n_heads, ATT_TQ, 1), F32), pltpu.VMEM((n_heads, ATT_TQ, 1), F32),
               pltpu.VMEM((n_heads, ATT_TQ, 2 * dv), F32), pltpu.VMEM((n_heads, ATT_TQ, ATT_TK), F32)]
    if moba:
        scratch += [pltpu.VMEM((LANES, n_heads * dq), F32), pltpu.VMEM((ATT_TQ, n_heads * dq), BF16)]
    return pl.pallas_call(
        functools.partial(_attn_kernel, n_heads=n_heads, dq=dq, dv=dv, moba=moba),
        grid=(BATCH, nq),
        in_specs=[
            pl.BlockSpec((ATT_TQ, q.shape[1]), lambda b, i: (b * nq + i, 0)),
            pl.BlockSpec((SEQ, n_heads * dq), lambda b, i: (b, 0)),
            pl.BlockSpec((SEQ, n_heads * 2 * dv), lambda b, i: (b, 0)),
        ],
        out_specs=pl.BlockSpec((ATT_TQ, n_heads * dv), lambda b, i: (b * nq + i, 0)),
        out_shape=jax.ShapeDtypeStruct((TOKENS, n_heads * dv), BF16),
        scratch_shapes=scratch,
        compiler_params=_params("parallel", "arbitrary"),
        name="moba_attn" if moba else "mla_attn",
    )(q, k, v)


SWA_STEP_BLOCKS = 4


def _swa_kernel(sink_ref, q_ref, kp_ref, kc_ref, vp_ref, vc_ref, o_ref, s_ref, p_ref, m_ref):
    hk = pl.program_id(1)
    n = pl.program_id(2)
    W = SWA_WINDOW
    G = SWA_GROUP
    nt = (((1,), (1,)), ((), ()))
    blocks = range(SWA_STEP_BLOCKS)
    kk = jnp.concatenate([kp_ref[...], kc_ref[...]], axis=0)
    vv = jnp.concatenate([vp_ref[...], vc_ref[...]], axis=0)
    r = lax.broadcasted_iota(jnp.int32, (G * W, 2 * W), 0) % W
    c = lax.broadcasted_iota(jnp.int32, (G * W, 2 * W), 1)
    rel = r + W - c
    band = (rel >= 0) & (rel < W)
    sink = jnp.concatenate([jnp.full((W, 1), sink_ref[hk * G + g], F32) for g in range(G)], axis=0)
    for blk in blocks:
        q = jnp.concatenate([q_ref[blk * W:(blk + 1) * W, g * LANES:(g + 1) * LANES] for g in range(G)], axis=0)
        s = lax.dot_general(q, kk[blk * W:(blk + 2) * W], nt, preferred_element_type=F32)
        mask = band & ((c >= W) | (n > 0)) if blk == 0 else band
        s_ref[blk] = jnp.where(mask, s, NEG_INF)
    for blk in blocks:
        m_ref[blk] = jnp.maximum(jnp.max(s_ref[blk], axis=-1, keepdims=True), sink)
    for blk in blocks:
        p_ref[blk] = jnp.exp(s_ref[blk] - m_ref[blk]).astype(BF16)
    for blk in blocks:
        pv = jnp.dot(p_ref[blk], vv[blk * W:(blk + 2) * W], preferred_element_type=F32)
        o = pv[:, :LANES] / (pv[:, LANES:] + jnp.exp(sink - m_ref[blk]))
        for g in range(G):
            o_ref[blk * W:(blk + 1) * W, g * LANES:(g + 1) * LANES] = o[g * W:(g + 1) * W].astype(o_ref.dtype)


def _swa(sinks, q, k, v):
    W = SWA_WINDOW
    step_rows = SWA_STEP_BLOCKS * W
    nsteps = SEQ // step_rows
    cur = lambda b, h, n: (b * nsteps + n, h)
    prev = lambda b, h, n: (b * (SEQ // W) + jnp.maximum(n * SWA_STEP_BLOCKS - 1, 0), h)
    return pl.pallas_call(
        _swa_kernel,
        grid=(BATCH, N_SWA_KV_HEADS, nsteps),
        in_specs=[
            pl.BlockSpec(memory_space=pltpu.SMEM),
            pl.BlockSpec((step_rows, SWA_GROUP * HEAD_DIM), cur),
            pl.BlockSpec((W, HEAD_DIM), prev),
            pl.BlockSpec((step_rows, HEAD_DIM), cur),
            pl.BlockSpec((W, 2 * HEAD_DIM), prev),
            pl.BlockSpec((step_rows, 2 * HEAD_DIM), cur),
        ],
        out_specs=pl.BlockSpec((step_rows, SWA_GROUP * HEAD_DIM), cur),
        out_shape=jax.ShapeDtypeStruct((TOKENS, N_SWA_HEADS * HEAD_DIM), BF16),
        scratch_shapes=[pltpu.VMEM((SWA_STEP_BLOCKS, SWA_GROUP * W, 2 * W), F32),
                        pltpu.VMEM((SWA_STEP_BLOCKS, SWA_GROUP * W, 2 * W), BF16),
                        pltpu.VMEM((SWA_STEP_BLOCKS, SWA_GROUP * W, 1), F32)],
        compiler_params=_params("parallel", "parallel", "arbitrary"),
        name="swa_attn",
    )(sinks, q, k, k, v, v)


OUT_TM = 512
W_MLA = N_MLA_HEADS * MLA_V_DIM
W_MOBA = N_MOBA_HEADS * HEAD_DIM
W_SWA = N_SWA_HEADS * HEAD_DIM
MIX_WIDTH = W_MLA + W_MOBA + W_SWA


def _outproj_kernel(x_ref, a_ref, b_ref, c_ref, w_ref, o_ref, w16_ref):
    @pl.when(pl.program_id(0) == 0)
    def _():
        def body(r, carry):
            sl = pl.ds(pl.multiple_of(r * NORM_ROWS, NORM_ROWS), NORM_ROWS)
            w16_ref[sl, :] = w_ref[sl, :].astype(BF16)
            return carry

        lax.fori_loop(0, MIX_WIDTH // NORM_ROWS, body, 0)

    acc = jnp.dot(a_ref[...], w16_ref[0:W_MLA, :], preferred_element_type=F32)
    acc += jnp.dot(b_ref[...], w16_ref[W_MLA:W_MLA + W_MOBA, :], preferred_element_type=F32)
    acc += jnp.dot(c_ref[...], w16_ref[W_MLA + W_MOBA:, :], preferred_element_type=F32)
    o_ref[...] = x_ref[...] + acc


def _outproj(x, a, b, c, w, layer):
    row = lambda i: (i, 0)
    return pl.pallas_call(
        _outproj_kernel,
        grid=(TOKENS // OUT_TM,),
        in_specs=[
            pl.BlockSpec((OUT_TM, D_MODEL), row),
            pl.BlockSpec((OUT_TM, W_MLA), row),
            pl.BlockSpec((OUT_TM, W_MOBA), row),
            pl.BlockSpec((OUT_TM, W_SWA), row),
            pl.BlockSpec((None, MIX_WIDTH, D_MODEL), lambda i: (layer, 0, 0), pipeline_mode=pl.Buffered(1)),
        ],
        out_specs=pl.BlockSpec((OUT_TM, D_MODEL), row),
        out_shape=jax.ShapeDtypeStruct((TOKENS, D_MODEL), F32),
        scratch_shapes=[pltpu.VMEM((MIX_WIDTH, D_MODEL), BF16)],
        compiler_params=_params("arbitrary"),
        name="out_proj",
    )(x, a, b, c, w)


def _rope_tables():
    def angles(d):
        half = d // 2
        inv_freq = 1.0 / (ROPE_THETA ** (jnp.arange(half, dtype=F32) * (2.0 / d)))
        return jnp.arange(SEQ, dtype=F32)[:, None] * inv_freq[None, :]

    ang = angles(HEAD_DIM)
    cos = jnp.concatenate([jnp.cos(ang), jnp.cos(ang)], axis=1)
    sin = jnp.concatenate([-jnp.sin(ang), jnp.sin(ang)], axis=1)
    angm = angles(MLA_ROPE_DIM)
    zero = jnp.zeros_like(angm)
    cosm = jnp.concatenate([jnp.cos(angm), zero, jnp.cos(angm), zero], axis=1)
    sinm = jnp.concatenate([-jnp.sin(angm), zero, jnp.sin(angm), zero], axis=1)
    return cos, sin, cosm, sinm


def _spread_rope_cols(w):
    half = MLA_ROPE_DIM // 2
    zero = jnp.zeros(w.shape[:-1] + (half,), w.dtype)
    return jnp.concatenate([w[..., :half], zero, w[..., half:], zero], axis=-1)


LAYOUT_ROWS = 256


def _layout_w_in_kernel(w_ref, o_ref):
    half = MLA_ROPE_DIM // 2
    w = w_ref[...].astype(BF16)
    gap = jnp.zeros((LAYOUT_ROWS, half), BF16)
    pad = jnp.zeros((LAYOUT_ROWS, Z_WIDTH - Z_SV - N_SWA_KV_HEADS * HEAD_DIM), BF16)
    o_ref[...] = jnp.concatenate([w[:, :Z_KROPE + half], gap, w[:, Z_KROPE + half:Z_KROPE + 2 * half], gap,
                                  w[:, Z_KROPE + 2 * half:], pad], axis=-1)


def _layout_w_in(w):
    in_width = w.shape[-1]
    return pl.pallas_call(
        _layout_w_in_kernel,
        grid=(DEPTH, D_MODEL // LAYOUT_ROWS),
        in_specs=[pl.BlockSpec((None, LAYOUT_ROWS, in_width), lambda l, i: (l, i, 0))],
        out_specs=pl.BlockSpec((None, LAYOUT_ROWS, Z_WIDTH), lambda l, i: (l, i, 0)),
        out_shape=jax.ShapeDtypeStruct((DEPTH, D_MODEL, Z_WIDTH), BF16),
        compiler_params=_params("parallel", "parallel"),
        name="layout_w_in",
    )(w)


def _layout_w_uq(w):
    w = w.astype(BF16).reshape(DEPTH, MLA_Q_LORA, N_MLA_HEADS, MLA_NOPE_DIM + MLA_ROPE_DIM)
    w = jnp.concatenate([w[..., :MLA_NOPE_DIM], _spread_rope_cols(w[..., MLA_NOPE_DIM:])], axis=-1)
    return w.reshape(DEPTH, MLA_Q_LORA, N_MLA_HEADS * MLA_QK_PAD)


def kernel(x, ffn1_norm, ffn1_w_gate, ffn1_w_up, ffn1_w_down, attn_norm, w_in, mla_q_norm, mla_w_uq, mla_kv_norm, mla_w_ukv, swa_sinks, w_out, ffn2_norm, ffn2_w_gate, ffn2_w_up, ffn2_w_down, final_norm):
    tabs = _rope_tables()
    x = x.reshape(TOKENS, D_MODEL)
    ffn1_g = ffn1_norm.reshape(DEPTH, 1, D_MODEL)
    ffn2_g = ffn2_norm.reshape(DEPTH, 1, D_MODEL)
    w_in16 = _layout_w_in(w_in)
    w_uq16 = _layout_w_uq(mla_w_uq)
    w_ukv16 = mla_w_ukv.astype(BF16)
    final_g = final_norm.reshape(1, D_MODEL)
    for l in range(DEPTH):
        x = _ffn(x, ffn1_g, ffn1_w_gate, ffn1_w_up, ffn1_w_down, l, final_g)
        (mla_q, mla_k, mla_v, mq, mk, mv, sq, sk, sv) = _prep(
            x, attn_norm[l], w_in16, tabs, mla_q_norm[l], mla_kv_norm[l], w_uq16, w_ukv16, l)
        o_mla = _attention(mla_q, mla_k, mla_v, N_MLA_HEADS, MLA_QK_PAD, MLA_V_DIM, moba=False)
        o_moba = _attention(mq, mk, mv, N_MOBA_HEADS, MOBA_QK_AUG, HEAD_DIM, moba=True)
        o_swa = _swa(swa_sinks[l], sq, sk, sv)
        x = _outproj(x, o_mla, o_moba, o_swa, w_out, l)
        x = _ffn(x, ffn2_g, ffn2_w_gate, ffn2_w_up, ffn2_w_down, l, final_g, post_norm=(l == DEPTH - 1))
    return x.reshape(BATCH, SEQ, D_MODEL)
```

```python
import functools

import jax
import jax.numpy as jnp
import numpy as np
from jax import lax
from jax.experimental import pallas as pl
from jax.experimental.pallas import tpu as pltpu

D_MODEL = 2048
BATCH = 2
SEQ = 4096
DEPTH = 2
TOKENS = BATCH * SEQ

HEAD_DIM = 128
N_MLA_HEADS = 4
MLA_Q_LORA = 512
MLA_KV_LORA = 256
MLA_NOPE_DIM = 128
MLA_ROPE_DIM = 64
MLA_V_DIM = 128
MLA_QK_PAD = 256
N_MOBA_HEADS = 4
MOBA_BLOCK = 256
MOBA_TOPK = 3
MOBA_QK_AUG = 256
N_SWA_HEADS = 8
N_SWA_KV_HEADS = 2
SWA_GROUP = N_SWA_HEADS // N_SWA_KV_HEADS
SWA_WINDOW = 128
D_FF = 5632
ROPE_THETA = 10000.0
NORM_EPS = 1e-6

LANES = 128
Z_WIDTH = 4096

Z_CQ = 0
Z_CKV = 512
Z_KROPE = 768
Z_MQ = 896
Z_MK = 1408
Z_MV = 1920
Z_SQ = 2432
Z_SK = 3456
Z_SV = 3712

VMEM_LIMIT = 56 * 1024 * 1024

BF16 = jnp.bfloat16
F32 = jnp.float32
NEG_INF = float("-inf")


def _params(*sem, flags=None):
    return pltpu.CompilerParams(dimension_semantics=sem, vmem_limit_bytes=VMEM_LIMIT, flags=flags)


FFN_TM = 1024
FFN_TF = 256
FFN_TAIL_CHUNKS = 2
FFN_DOWN_TN = 512
NORM_ROWS = 128


def _rmsnorm_rows(x, g):
    ms = jnp.mean(x * x, axis=-1, keepdims=True)
    return x * lax.rsqrt(ms + NORM_EPS) * g


def _norm_into(x_ref, g_ref, h_ref, rows, copy_ref=None):
    g = g_ref[...]

    def body(r, carry):
        sl = pl.ds(pl.multiple_of(r * NORM_ROWS, NORM_ROWS), NORM_ROWS)
        x = x_ref[sl, :]
        h_ref[sl, :] = _rmsnorm_rows(x, g).astype(BF16)
        if copy_ref is not None:
            copy_ref[sl, :] = x
        return carry

    lax.fori_loop(0, rows // NORM_ROWS, body, 0)


def _ffn_chunk(h, wg, wu, wd_cols, o_ref):
    gate = jnp.dot(h, wg, preferred_element_type=F32)
    up = jnp.dot(h, wu, preferred_element_type=F32)
    act = (gate * jax.nn.sigmoid(gate) * (0.5 * up)).astype(BF16)
    for c in range(D_MODEL // FFN_DOWN_TN):
        cols = slice(c * FFN_DOWN_TN, (c + 1) * FFN_DOWN_TN)
        o_ref[:, cols] += jnp.dot(act, wd_cols(cols), preferred_element_type=F32)


def _norm_in_place(o_ref, g_ref, rows):
    g = g_ref[...]

    def body(r, carry):
        sl = pl.ds(pl.multiple_of(r * NORM_ROWS, NORM_ROWS), NORM_ROWS)
        o_ref[sl, :] = _rmsnorm_rows(o_ref[sl, :], g)
        return carry

    lax.fori_loop(0, rows // NORM_ROWS, body, 0)


def _ffn_head_kernel(x_ref, g_ref, post_g_ref, wg_ref, wu_ref, wd_ref, o_ref, wg16_ref, wu16_ref, wd16_ref, h_ref,
                     *, post_norm):
    @pl.when(pl.program_id(0) == 0)
    def _():
        _norm_into(x_ref, g_ref, h_ref, FFN_TM, copy_ref=o_ref)

    wg16_ref[...] = wg_ref[...].astype(BF16)
    wu16_ref[...] = wu_ref[...].astype(BF16)
    wd16_ref[...] = wd_ref[...].astype(BF16)
    _ffn_chunk(h_ref[...], wg16_ref[...], wu16_ref[...], lambda cols: wd16_ref[:, cols], o_ref)

    if post_norm:
        @pl.when(pl.program_id(0) == pl.num_programs(0) - 1)
        def _():
            _norm_in_place(o_ref, post_g_ref, FFN_TM)


def _ffn_tail_kernel(head_out_ref, x_ref, g_ref, post_g_ref, wg16_ref, wu16_ref, wd16_ref, o_ref, h_ref,
                     *, post_norm):
    del head_out_ref

    @pl.when(pl.program_id(1) == 0)
    def _():
        _norm_into(x_ref, g_ref, h_ref, FFN_TM, copy_ref=o_ref)

    for c in range(FFN_TAIL_CHUNKS):
        rows = slice(c * FFN_TF, (c + 1) * FFN_TF)
        _ffn_chunk(h_ref[...], wg16_ref[c], wu16_ref[c],
                   lambda cols, rows=rows: wd16_ref[rows, cols], o_ref)

    if post_norm:
        @pl.when(pl.program_id(1) == pl.num_programs(1) - 1)
        def _():
            _norm_in_place(o_ref, post_g_ref, FFN_TM)


def _ffn(x, g, wg, wu, wd, layer, post_g, post_norm=False):
    n_chunks = D_FF // FFN_TF
    out, wg16, wu16, wd16 = pl.pallas_call(
        functools.partial(_ffn_head_kernel, post_norm=post_norm),
        grid=(n_chunks,),
        in_specs=[
            pl.BlockSpec((FFN_TM, D_MODEL), lambda j: (0, 0)),
            pl.BlockSpec((None, 1, D_MODEL), lambda j: (layer, 0, 0)),
            pl.BlockSpec((1, D_MODEL), lambda j: (0, 0)),
            pl.BlockSpec((None, D_MODEL, FFN_TF), lambda j: (layer, 0, j)),
            pl.BlockSpec((None, D_MODEL, FFN_TF), lambda j: (layer, 0, j)),
            pl.BlockSpec((None, FFN_TF, D_MODEL), lambda j: (layer, j, 0)),
        ],
        out_specs=[
            pl.BlockSpec((FFN_TM, D_MODEL), lambda j: (0, 0)),
            pl.BlockSpec((None, D_MODEL, FFN_TF), lambda j: (j, 0, 0)),
            pl.BlockSpec((None, D_MODEL, FFN_TF), lambda j: (j, 0, 0)),
            pl.BlockSpec((FFN_TF, D_MODEL), lambda j: (j, 0)),
        ],
        out_shape=[
            jax.ShapeDtypeStruct((TOKENS, D_MODEL), F32),
            jax.ShapeDtypeStruct((n_chunks, D_MODEL, FFN_TF), BF16),
            jax.ShapeDtypeStruct((n_chunks, D_MODEL, FFN_TF), BF16),
            jax.ShapeDtypeStruct((D_FF, D_MODEL), BF16),
        ],
        scratch_shapes=[pltpu.VMEM((FFN_TM, D_MODEL), BF16)],
        compiler_params=_params("arbitrary"),
        name="ffn_head",
    )(x, g, post_g, wg, wu, wd)
    return pl.pallas_call(
        functools.partial(_ffn_tail_kernel, post_norm=post_norm),
        grid=(TOKENS // FFN_TM - 1, n_chunks // FFN_TAIL_CHUNKS),
        in_specs=[
            pl.BlockSpec(memory_space=pl.ANY),
            pl.BlockSpec((FFN_TM, D_MODEL), lambda i, j: (i + 1, 0)),
            pl.BlockSpec((None, 1, D_MODEL), lambda i, j: (layer, 0, 0)),
            pl.BlockSpec((1, D_MODEL), lambda i, j: (0, 0)),
            pl.BlockSpec((FFN_TAIL_CHUNKS, D_MODEL, FFN_TF), lambda i, j: (j, 0, 0)),
            pl.BlockSpec((FFN_TAIL_CHUNKS, D_MODEL, FFN_TF), lambda i, j: (j, 0, 0)),
            pl.BlockSpec((FFN_TAIL_CHUNKS * FFN_TF, D_MODEL), lambda i, j: (j, 0)),
        ],
        out_specs=pl.BlockSpec((FFN_TM, D_MODEL), lambda i, j: (i + 1, 0)),
        out_shape=jax.ShapeDtypeStruct((TOKENS, D_MODEL), F32),
        input_output_aliases={0: 0},
        scratch_shapes=[pltpu.VMEM((FFN_TM, D_MODEL), BF16)],
        compiler_params=_params("parallel", "arbitrary"),
        name="ffn_tail",
    )(out, x, g, post_g, wg16, wu16, wd16)


PROJ_TN = 512
PREP_TM = 512


def _project_into(x_ref, g_ref, w_ref, z_ref, h_ref):
    _norm_into(x_ref, g_ref, h_ref, PREP_TM)
    nt = (((1,), (1,)), ((), ()))
    for c in range(Z_WIDTH // PROJ_TN):
        cols = slice(c * PROJ_TN, (c + 1) * PROJ_TN)
        z_ref[:, cols] = lax.dot_general(h_ref[...], w_ref[cols, :], nt, preferred_element_type=F32)


def _rope(x, cos, sin_signed):
    return x * cos + pltpu.roll(x, LANES // 2, axis=1) * sin_signed


def _prep_kernel(x_ref, g_ref, w_ref, cos_ref, sin_ref, cosm_ref, sinm_ref, qn_ref, kvn_ref,
                 wuq_ref, wukv_ref,
                 mlaq_ref, mlak_ref, mlav_ref, mq_ref, mk_ref, mv_ref,
                 sq_ref, sk_ref, sv_ref, z_ref, h_ref):
    _project_into(x_ref, g_ref, w_ref, z_ref, h_ref)
    cos = cos_ref[...]
    sin = sin_ref[...]
    cosm = cosm_ref[...]
    sinm = sinm_ref[...]
    mla_scale = (MLA_NOPE_DIM + MLA_ROPE_DIM) ** -0.5
    scale = HEAD_DIM ** -0.5

    cq = _rmsnorm_rows(z_ref[:, Z_CQ:Z_CQ + MLA_Q_LORA], qn_ref[...]).astype(BF16)
    q = jnp.dot(cq, wuq_ref[...], preferred_element_type=F32)
    for hd in range(N_MLA_HEADS):
        base = hd * MLA_QK_PAD
        mlaq_ref[:, base:base + LANES] = (q[:, base:base + LANES] * mla_scale).astype(BF16)
        pe = _rope(q[:, base + LANES:base + 2 * LANES], cosm, sinm)
        mlaq_ref[:, base + LANES:base + 2 * LANES] = (pe * mla_scale).astype(BF16)

    ones = jnp.ones((PREP_TM, LANES), BF16)
    ckv = _rmsnorm_rows(z_ref[:, Z_CKV:Z_CKV + MLA_KV_LORA], kvn_ref[...]).astype(BF16)
    kv = jnp.dot(ckv, wukv_ref[...], preferred_element_type=F32)
    kpe = _rope(z_ref[:, Z_KROPE:Z_KROPE + LANES], cosm, sinm).astype(BF16)
    for hd in range(N_MLA_HEADS):
        base = hd * MLA_QK_PAD
        mlak_ref[:, base:base + LANES] = kv[:, base:base + LANES].astype(BF16)
        mlak_ref[:, base + LANES:base + 2 * LANES] = kpe
        mlav_ref[:, base:base + LANES] = kv[:, base + LANES:base + 2 * LANES].astype(BF16)
        mlav_ref[:, base + LANES:base + 2 * LANES] = ones

    pos = (pl.program_id(0) % (SEQ // PREP_TM)) * PREP_TM + lax.broadcasted_iota(jnp.int32, (PREP_TM, LANES), 0)
    lane = lax.broadcasted_iota(jnp.int32, (PREP_TM, LANES), 1)
    block_onehot = jnp.where(pos // MOBA_BLOCK == lane, 1.0, 0.0).astype(BF16)
    for hd in range(N_MOBA_HEADS):
        c = hd * LANES
        mq_ref[:, c:c + LANES] = (_rope(z_ref[:, Z_MQ + c:Z_MQ + c + LANES], cos, sin) * scale).astype(BF16)
        mk_ref[:, 2 * c:2 * c + LANES] = _rope(z_ref[:, Z_MK + c:Z_MK + c + LANES], cos, sin).astype(BF16)
        mk_ref[:, 2 * c + LANES:2 * c + 2 * LANES] = block_onehot
        mv_ref[:, 2 * c:2 * c + LANES] = z_ref[:, Z_MV + c:Z_MV + c + LANES].astype(BF16)
        mv_ref[:, 2 * c + LANES:2 * c + 2 * LANES] = ones

    for hd in range(N_SWA_HEADS):
        c = hd * LANES
        sq_ref[:, c:c + LANES] = (_rope(z_ref[:, Z_SQ + c:Z_SQ + c + LANES], cos, sin) * scale).astype(BF16)
    for hd in range(N_SWA_KV_HEADS):
        c = hd * LANES
        sk_ref[:, c:c + LANES] = _rope(z_ref[:, Z_SK + c:Z_SK + c + LANES], cos, sin).astype(BF16)
        sv_ref[:, 2 * c:2 * c + LANES] = z_ref[:, Z_SV + c:Z_SV + c + LANES].astype(BF16)
        sv_ref[:, 2 * c + LANES:2 * c + 2 * LANES] = ones


def _prep(x, g, w_in, tabs, qn, kvn, wuq, wukv, layer):
    nblk = SEQ // PREP_TM
    row = lambda i: (i, 0)
    tab = lambda i: (i % nblk, 0)
    const = lambda i: (0, 0)
    of_layer = lambda i: (layer, 0, 0)
    widths = [N_MLA_HEADS * MLA_QK_PAD, N_MLA_HEADS * MLA_QK_PAD, N_MLA_HEADS * 2 * MLA_V_DIM,
              N_MOBA_HEADS * HEAD_DIM, N_MOBA_HEADS * MOBA_QK_AUG, N_MOBA_HEADS * 2 * HEAD_DIM,
              N_SWA_HEADS * HEAD_DIM, N_SWA_KV_HEADS * HEAD_DIM, N_SWA_KV_HEADS * 2 * HEAD_DIM]
    return pl.pallas_call(
        _prep_kernel,
        grid=(TOKENS // PREP_TM,),
        in_specs=[
            pl.BlockSpec((PREP_TM, D_MODEL), row),
            pl.BlockSpec((1, D_MODEL), const),
            pl.BlockSpec((None, Z_WIDTH, D_MODEL), of_layer, pipeline_mode=pl.Buffered(1)),
            pl.BlockSpec((PREP_TM, LANES), tab),
            pl.BlockSpec((PREP_TM, LANES), tab),
            pl.BlockSpec((PREP_TM, LANES), tab),
            pl.BlockSpec((PREP_TM, LANES), tab),
            pl.BlockSpec((1, MLA_Q_LORA), const),
            pl.BlockSpec((1, MLA_KV_LORA), const),
            pl.BlockSpec((None, MLA_Q_LORA, N_MLA_HEADS * MLA_QK_PAD), of_layer),
            pl.BlockSpec((None, MLA_KV_LORA, N_MLA_HEADS * MLA_QK_PAD), of_layer),
        ],
        out_specs=[pl.BlockSpec((PREP_TM, w), row) for w in widths],
        out_shape=[jax.ShapeDtypeStruct((TOKENS, w), BF16) for w in widths],
        scratch_shapes=[pltpu.VMEM((PREP_TM, Z_WIDTH), F32), pltpu.VMEM((PREP_TM, D_MODEL), BF16)],
        compiler_params=_params("parallel"),
        name="mixer_prep",
    )(x, g.reshape(1, D_MODEL), w_in, *tabs, qn.reshape(1, -1), kvn.reshape(1, -1), wuq, wukv)


ATT_TQ = MOBA_BLOCK
ATT_TK = 2 * MOBA_BLOCK


def _split_bf16(x):
    hi = x.astype(BF16)
    lo = (x - hi.astype(F32)).astype(BF16)
    return hi, lo


MASKED = -1e30


def _moba_gate_logits(q, kbar, qi):
    nblk = SEQ // MOBA_BLOCK
    nt = (((1,), (1,)), ((), ()))
    kb_hi, kb_lo = _split_bf16(kbar)
    gate = (lax.dot_general(kb_hi, q, nt, preferred_element_type=F32)
            + lax.dot_general(kb_lo, q, nt, preferred_element_type=F32))[:nblk]
    blk = lax.broadcasted_iota(jnp.int32, gate.shape, 0)
    gate = jnp.where(blk < qi, gate, NEG_INF)
    rank = jnp.zeros(gate.shape, jnp.int32)
    for other in range(nblk):
        row = gate[other:other + 1, :]
        beats = (row > gate) | ((row == gate) & (other < blk))
        rank = rank + jnp.where(beats, 1, 0)
    keep = ((rank < MOBA_TOPK) & (blk < qi)) | (blk == qi)
    logit_t = jnp.where(keep, 0.0, MASKED)
    logit_t = jnp.concatenate([logit_t, jnp.full((LANES - nblk, gate.shape[1]), MASKED, F32)], axis=0)
    return logit_t.T


def _attn_kernel(q_ref, k_ref, v_ref, o_ref, m_ref, mnext_ref, acc_ref, s_ref, *moba_scratch,
                 n_heads, dq, dv, moba):
    qi = pl.program_id(1)
    nt = (((1,), (1,)), ((), ()))

    if moba:
        kbar_ref, qaug_ref = moba_scratch

        @pl.when(qi == 0)
        def _():
            kbar_ref[...] = jnp.zeros_like(kbar_ref)

            def body(b, carry):
                sl = pl.ds(pl.multiple_of(b * MOBA_BLOCK, MOBA_BLOCK), MOBA_BLOCK)
                kbar_ref[pl.ds(b, 1), :] = jnp.mean(k_ref[sl, :].astype(F32), axis=0, keepdims=True)
                return carry

            lax.fori_loop(0, SEQ // MOBA_BLOCK, body, 0)

        for hd in range(n_heads):
            qh = q_ref[:, hd * HEAD_DIM:(hd + 1) * HEAD_DIM]
            logits = _moba_gate_logits(qh, kbar_ref[:, hd * dq:hd * dq + HEAD_DIM], qi)
            qaug_ref[:, hd * dq:hd * dq + HEAD_DIM] = qh
            qaug_ref[:, hd * dq + HEAD_DIM:(hd + 1) * dq] = logits.astype(BF16)
        q_src = qaug_ref
    else:
        q_src = q_ref

    def tile(t):
        return pl.ds(pl.multiple_of(t * ATT_TK, ATT_TK), ATT_TK)

    r = lax.broadcasted_iota(jnp.int32, (ATT_TQ, ATT_TQ), 0)
    c = lax.broadcasted_iota(jnp.int32, (ATT_TQ, ATT_TQ), 1)
    tri = c <= r
    if ATT_TK == ATT_TQ:
        t_diag = qi
        mask = tri
    else:
        t_diag = qi // 2
        odd = (qi % 2) == 1
        mask = jnp.concatenate([tri | odd, tri & odd], axis=1)
    dva = 2 * dv

    def scores(t, hd):
        return lax.dot_general(q_src[:, hd * dq:(hd + 1) * dq], k_ref[tile(t), hd * dq:(hd + 1) * dq], nt,
                               preferred_element_type=F32)

    def weighted_values(p, t, hd):
        return jnp.dot(p.astype(BF16), v_ref[tile(t), hd * dva:(hd + 1) * dva], preferred_element_type=F32)

    heads = range(n_heads)
    for hd in heads:
        s = jnp.where(mask, scores(t_diag, hd), NEG_INF)
        m = jnp.max(s, axis=-1, keepdims=True)
        m_ref[hd] = m
        acc_ref[hd] = weighted_values(jnp.exp(s - m), t_diag, hd)

    def body(t, carry):
        t_next = jnp.minimum(t + 1, t_diag - 1)
        for hd in heads:
            m_cur = mnext_ref[hd]
            p = jnp.exp(s_ref[hd] - m_cur)
            s = scores(t_next, hd)
            s_ref[hd] = s
            mnext_ref[hd] = jnp.maximum(m_cur, jnp.max(s, axis=-1, keepdims=True))
            alpha = jnp.exp(m_ref[hd] - m_cur)
            acc_ref[hd] = alpha * acc_ref[hd] + weighted_values(p, t, hd)
            m_ref[hd] = m_cur
        return carry

    for hd in heads:
        s = scores(0, hd)
        s_ref[hd] = s
        mnext_ref[hd] = jnp.maximum(m_ref[hd], jnp.max(s, axis=-1, keepdims=True))
    lax.fori_loop(0, t_diag, body, 0)
    for hd in heads:
        acc = acc_ref[hd]
        o_ref[:, hd * dv:(hd + 1) * dv] = (acc[:, :dv] / acc[:, dv:]).astype(o_ref.dtype)


def _attention(q, k, v, n_heads, dq, dv, moba):
    nq = SEQ // ATT_TQ
    scratch = [pltpu.VMEM((n_heads, ATT_TQ, 1), F32), pltpu.VMEM((n_heads, ATT_TQ, 1), F32),
               pltpu.VMEM((n_heads, ATT_TQ, 2 * dv), F32), pltpu.VMEM((n_heads, ATT_TQ, ATT_TK), F32)]
    if moba:
        scratch += [pltpu.VMEM((LANES, n_heads * dq), F32), pltpu.VMEM((ATT_TQ, n_heads * dq), BF16)]
    return pl.pallas_call(
        functools.partial(_attn_kernel, n_heads=n_heads, dq=dq, dv=dv, moba=moba),
        grid=(BATCH, nq),
        in_specs=[
            pl.BlockSpec((ATT_TQ, q.shape[1]), lambda b, i: (b * nq + i, 0)),
            pl.BlockSpec((SEQ, n_heads * dq), lambda b, i: (b, 0)),
            pl.BlockSpec((SEQ, n_heads * 2 * dv), lambda b, i: (b, 0)),
        ],
        out_specs=pl.BlockSpec((ATT_TQ, n_heads * dv), lambda b, i: (b * nq + i, 0)),
        out_shape=jax.ShapeDtypeStruct((TOKENS, n_heads * dv), BF16),
        scratch_shapes=scratch,
        compiler_params=_params("parallel", "arbitrary"),
        name="moba_attn" if moba else "mla_attn",
    )(q, k, v)


SWA_STEP_BLOCKS = 4


OUT_TN = 512
W_MLA = N_MLA_HEADS * MLA_V_DIM
W_MOBA = N_MOBA_HEADS * HEAD_DIM
W_SWA = N_SWA_HEADS * HEAD_DIM
MIX_WIDTH = W_MLA + W_MOBA + W_SWA
SWA_STEP_ROWS = SWA_STEP_BLOCKS * SWA_WINDOW


def _swa_out_kernel(sink_ref, x_ref, a_ref, b_ref, q_ref, kp_ref, kc_ref, vp_ref, vc_ref, w_ref, o_ref,
                    w16_ref, s_ref, p_ref, m_ref, c_ref):
    step = pl.program_id(0)
    n = step % (SEQ // SWA_STEP_ROWS)
    W = SWA_WINDOW
    G = SWA_GROUP
    nt = (((1,), (1,)), ((), ()))
    blocks = range(SWA_STEP_BLOCKS)
    col_chunks = [slice(c * OUT_TN, (c + 1) * OUT_TN) for c in range(D_MODEL // OUT_TN)]

    @pl.when(step == 0)
    def _():
        def body(r, carry):
            sl = pl.ds(pl.multiple_of(r * NORM_ROWS, NORM_ROWS), NORM_ROWS)
            w16_ref[sl, :] = w_ref[sl, :].astype(BF16)
            return carry

        lax.fori_loop(0, MIX_WIDTH // NORM_ROWS, body, 0)

    r = lax.broadcasted_iota(jnp.int32, (G * W, 2 * W), 0) % W
    c = lax.broadcasted_iota(jnp.int32, (G * W, 2 * W), 1)
    rel = r + W - c
    band = (rel >= 0) & (rel < W)

    def project(lhs, w_rows, cols, first):
        part = jnp.dot(lhs, w16_ref[w_rows, cols], preferred_element_type=F32)
        if first:
            o_ref[:, cols] = x_ref[:, cols] + part
        else:
            o_ref[:, cols] += part

    fillers = [
        [functools.partial(project, a_ref[...], slice(0, W_MLA), cols, True) for cols in col_chunks],
        [functools.partial(project, b_ref[...], slice(W_MLA, W_MLA + W_MOBA), cols, False) for cols in col_chunks],
    ]

    for hk in range(N_SWA_KV_HEADS):
        kcols = slice(hk * HEAD_DIM, (hk + 1) * HEAD_DIM)
        vcols = slice(hk * 2 * HEAD_DIM, (hk + 1) * 2 * HEAD_DIM)
        kk = jnp.concatenate([kp_ref[:, kcols], kc_ref[:, kcols]], axis=0)
        vv = jnp.concatenate([vp_ref[:, vcols], vc_ref[:, vcols]], axis=0)
        sink = jnp.concatenate([jnp.full((W, 1), sink_ref[hk * G + g], F32) for g in range(G)], axis=0)
        fill = fillers[hk]
        for blk in blocks:
            q = jnp.concatenate([q_ref[blk * W:(blk + 1) * W, (hk * G + g) * LANES:(hk * G + g + 1) * LANES]
                                 for g in range(G)], axis=0)
            s = lax.dot_general(q, kk[blk * W:(blk + 2) * W], nt, preferred_element_type=F32)
            mask = band & ((c >= W) | (n > 0)) if blk == 0 else band
            s_ref[blk] = jnp.where(mask, s, NEG_INF)
        fill[0]()
        for blk in blocks:
            m_ref[blk] = jnp.maximum(jnp.max(s_ref[blk], axis=-1, keepdims=True), sink)
        fill[1]()
        for blk in blocks:
            p_ref[blk] = jnp.exp(s_ref[blk] - m_ref[blk]).astype(BF16)
        fill[2]()
        for blk in blocks:
            pv = jnp.dot(p_ref[blk], vv[blk * W:(blk + 2) * W], preferred_element_type=F32)
            o = pv[:, :LANES] / (pv[:, LANES:] + jnp.exp(sink - m_ref[blk]))
            for g in range(G):
                c_ref[blk * W:(blk + 1) * W, (hk * G + g) * LANES:(hk * G + g + 1) * LANES] = (
                    o[g * W:(g + 1) * W].astype(BF16))
        fill[3]()

    for cols in col_chunks:
        project(c_ref[...], slice(W_MLA + W_MOBA, MIX_WIDTH), cols, False)


def _swa_out(sinks, x, a, b, q, k, v, w, layer):
    W = SWA_WINDOW
    nsteps = SEQ // SWA_STEP_ROWS
    cur = lambda i: (i, 0)
    prev = lambda i: ((i // nsteps) * (SEQ // W) + jnp.maximum((i % nsteps) * SWA_STEP_BLOCKS - 1, 0), 0)
    kw = N_SWA_KV_HEADS * HEAD_DIM
    return pl.pallas_call(
        _swa_out_kernel,
        grid=(TOKENS // SWA_STEP_ROWS,),
        in_specs=[
            pl.BlockSpec(memory_space=pltpu.SMEM),
            pl.BlockSpec((SWA_STEP_ROWS, D_MODEL), cur),
            pl.BlockSpec((SWA_STEP_ROWS, W_MLA), cur),
            pl.BlockSpec((SWA_STEP_ROWS, W_MOBA), cur),
            pl.BlockSpec((SWA_STEP_ROWS, W_SWA), cur),
            pl.BlockSpec((W, kw), prev),
            pl.BlockSpec((SWA_STEP_ROWS, kw), cur),
            pl.BlockSpec((W, 2 * kw), prev),
            pl.BlockSpec((SWA_STEP_ROWS, 2 * kw), cur),
            pl.BlockSpec((None, MIX_WIDTH, D_MODEL), lambda i: (layer, 0, 0), pipeline_mode=pl.Buffered(1)),
        ],
        out_specs=pl.BlockSpec((SWA_STEP_ROWS, D_MODEL), cur),
        out_shape=jax.ShapeDtypeStruct((TOKENS, D_MODEL), F32),
        scratch_shapes=[pltpu.VMEM((MIX_WIDTH, D_MODEL), BF16),
                        pltpu.VMEM((SWA_STEP_BLOCKS, SWA_GROUP * W, 2 * W), F32),
                        pltpu.VMEM((SWA_STEP_BLOCKS, SWA_GROUP * W, 2 * W), BF16),
                        pltpu.VMEM((SWA_STEP_BLOCKS, SWA_GROUP * W, 1), F32),
                        pltpu.VMEM((SWA_STEP_ROWS, W_SWA), BF16)],
        compiler_params=_params("arbitrary"),
        name="swa_out",
    )(sinks, x, a, b, q, k, k, v, v, w)


def _rope_tables():
    def angles(d):
        half = d // 2
        inv_freq = 1.0 / (ROPE_THETA ** (jnp.arange(half, dtype=F32) * (2.0 / d)))
        return jnp.arange(SEQ, dtype=F32)[:, None] * inv_freq[None, :]

    ang = angles(HEAD_DIM)
    cos = jnp.concatenate([jnp.cos(ang), jnp.cos(ang)], axis=1)
    sin = jnp.concatenate([-jnp.sin(ang), jnp.sin(ang)], axis=1)
    angm = angles(MLA_ROPE_DIM)
    zero = jnp.zeros_like(angm)
    cosm = jnp.concatenate([jnp.cos(angm), zero, jnp.cos(angm), zero], axis=1)
    sinm = jnp.concatenate([-jnp.sin(angm), zero, jnp.sin(angm), zero], axis=1)
    return cos, sin, cosm, sinm


def _spread_rope_cols(w):
    half = MLA_ROPE_DIM // 2
    zero = jnp.zeros(w.shape[:-1] + (half,), w.dtype)
    return jnp.concatenate([w[..., :half], zero, w[..., half:], zero], axis=-1)


LAYOUT_COLS = 512


def _layout_w_in_kernel(w_ref, o_ref):
    half = MLA_ROPE_DIM // 2
    src = Z_KROPE + 2 * half
    o_ref[0:Z_KROPE + half, :] = w_ref[0:Z_KROPE + half, :].astype(BF16)
    o_ref[Z_KROPE + half:Z_KROPE + 2 * half, :] = jnp.zeros((half, LAYOUT_COLS), BF16)
    o_ref[Z_KROPE + 2 * half:Z_KROPE + 3 * half, :] = w_ref[Z_KROPE + half:src, :].astype(BF16)
    o_ref[Z_KROPE + 3 * half:Z_MQ, :] = jnp.zeros((half, LAYOUT_COLS), BF16)
    n_rest = w_ref.shape[0] - src
    o_ref[Z_MQ:Z_MQ + n_rest, :] = w_ref[src:, :].astype(BF16)
    o_ref[Z_MQ + n_rest:, :] = jnp.zeros((Z_WIDTH - Z_MQ - n_rest, LAYOUT_COLS), BF16)


def _layout_w_in(w):
    wt = jnp.swapaxes(w, 1, 2)
    return pl.pallas_call(
        _layout_w_in_kernel,
        grid=(DEPTH, D_MODEL // LAYOUT_COLS),
        in_specs=[pl.BlockSpec((None, wt.shape[1], LAYOUT_COLS), lambda l, i: (l, 0, i))],
        out_specs=pl.BlockSpec((None, Z_WIDTH, LAYOUT_COLS), lambda l, i: (l, 0, i)),
        out_shape=jax.ShapeDtypeStruct((DEPTH, Z_WIDTH, D_MODEL), BF16),
        compiler_params=_params("parallel", "parallel"),
        name="layout_w_in",
    )(wt)


def _layout_w_uq(w):
    w = w.astype(BF16).reshape(DEPTH, MLA_Q_LORA, N_MLA_HEADS, MLA_NOPE_DIM + MLA_ROPE_DIM)
    w = jnp.concatenate([w[..., :MLA_NOPE_DIM], _spread_rope_cols(w[..., MLA_NOPE_DIM:])], axis=-1)
    return w.reshape(DEPTH, MLA_Q_LORA, N_MLA_HEADS * MLA_QK_PAD)


def kernel(x, ffn1_norm, ffn1_w_gate, ffn1_w_up, ffn1_w_down, attn_norm, w_in, mla_q_norm, mla_w_uq, mla_kv_norm, mla_w_ukv, swa_sinks, w_out, ffn2_norm, ffn2_w_gate, ffn2_w_up, ffn2_w_down, final_norm):
    tabs = _rope_tables()
    x = x.reshape(TOKENS, D_MODEL)
    ffn1_g = ffn1_norm.reshape(DEPTH, 1, D_MODEL)
    ffn2_g = ffn2_norm.reshape(DEPTH, 1, D_MODEL)
    w_in16 = _layout_w_in(w_in)
    w_uq16 = _layout_w_uq(mla_w_uq)
    w_ukv16 = mla_w_ukv.astype(BF16)
    final_g = final_norm.reshape(1, D_MODEL)
    for l in range(DEPTH):
        x = _ffn(x, ffn1_g, ffn1_w_gate, ffn1_w_up, ffn1_w_down, l, final_g)
        (mla_q, mla_k, mla_v, mq, mk, mv, sq, sk, sv) = _prep(
            x, attn_norm[l], w_in16, tabs, mla_q_norm[l], mla_kv_norm[l], w_uq16, w_ukv16, l)
        o_mla = _attention(mla_q, mla_k, mla_v, N_MLA_HEADS, MLA_QK_PAD, MLA_V_DIM, moba=False)
        o_moba = _attention(mq, mk, mv, N_MOBA_HEADS, MOBA_QK_AUG, HEAD_DIM, moba=True)
        x = _swa_out(swa_sinks[l], x, o_mla, o_moba, sq, sk, sv, w_out, l)
        x = _ffn(x, ffn2_g, ffn2_w_gate, ffn2_w_up, ffn2_w_down, l, final_g, post_norm=(l == DEPTH - 1))
    return x.reshape(BATCH, SEQ, D_MODEL)
```

```python
import functools

import jax
import jax.numpy as jnp
import numpy as np
from jax import lax
from jax.experimental import pallas as pl
from jax.experimental.pallas import tpu as pltpu

D_MODEL = 2048
BATCH = 2
SEQ = 4096
DEPTH = 2
TOKENS = BATCH * SEQ

HEAD_DIM = 128
N_MLA_HEADS = 4
MLA_Q_LORA = 512
MLA_KV_LORA = 256
MLA_NOPE_DIM = 128
MLA_ROPE_DIM = 64
MLA_V_DIM = 128
MLA_QK_PAD = 256
N_MOBA_HEADS = 4
MOBA_BLOCK = 256
MOBA_TOPK = 3
MOBA_QK_AUG = 256
N_SWA_HEADS = 8
N_SWA_KV_HEADS = 2
SWA_GROUP = N_SWA_HEADS // N_SWA_KV_HEADS
SWA_WINDOW = 128
D_FF = 5632
ROPE_THETA = 10000.0
NORM_EPS = 1e-6

LANES = 128
Z_WIDTH = 4096

Z_CQ = 0
Z_CKV = 512
Z_KROPE = 768
Z_MQ = 896
Z_MK = 1408
Z_MV = 1920
Z_SQ = 2432
Z_SK = 3456
Z_SV = 3712

VMEM_LIMIT = 56 * 1024 * 1024

BF16 = jnp.bfloat16
F32 = jnp.float32
NEG_INF = float("-inf")


def _params(*sem, flags=None):
    return pltpu.CompilerParams(dimension_semantics=sem, vmem_limit_bytes=VMEM_LIMIT, flags=flags)


FFN_TM = 1024
FFN_TF = 256
FFN_TAIL_CHUNKS = 2
FFN_DOWN_TN = 512
NORM_ROWS = 128


def _rmsnorm_rows(x, g):
    ms = jnp.mean(x * x, axis=-1, keepdims=True)
    return x * lax.rsqrt(ms + NORM_EPS) * g


def _norm_into(x_ref, g_ref, h_ref, rows, copy_ref=None):
    g = g_ref[...]

    def body(r, carry):
        sl = pl.ds(pl.multiple_of(r * NORM_ROWS, NORM_ROWS), NORM_ROWS)
        x = x_ref[sl, :]
        h_ref[sl, :] = _rmsnorm_rows(x, g).astype(BF16)
        if copy_ref is not None:
            copy_ref[sl, :] = x
        return carry

    lax.fori_loop(0, rows // NORM_ROWS, body, 0)


def _ffn_chunk(h, wg, wu, wd_cols, o_ref):
    gate = jnp.dot(h, wg, preferred_element_type=F32)
    up = jnp.dot(h, wu, preferred_element_type=F32)
    act = (gate * jax.nn.sigmoid(gate) * (0.5 * up)).astype(BF16)
    for c in range(D_MODEL // FFN_DOWN_TN):
        cols = slice(c * FFN_DOWN_TN, (c + 1) * FFN_DOWN_TN)
        o_ref[:, cols] += jnp.dot(act, wd_cols(cols), preferred_element_type=F32)


def _norm_in_place(o_ref, g_ref, rows):
    g = g_ref[...]

    def body(r, carry):
        sl = pl.ds(pl.multiple_of(r * NORM_ROWS, NORM_ROWS), NORM_ROWS)
        o_ref[sl, :] = _rmsnorm_rows(o_ref[sl, :], g)
        return carry

    lax.fori_loop(0, rows // NORM_ROWS, body, 0)


def _ffn_head_kernel(x_ref, g_ref, post_g_ref, wg_ref, wu_ref, wd_ref, o_ref, wg16_ref, wu16_ref, wd16_ref, h_ref,
                     *, post_norm):
    @pl.when(pl.program_id(0) == 0)
    def _():
        _norm_into(x_ref, g_ref, h_ref, FFN_TM, copy_ref=o_ref)

    wg16_ref[...] = wg_ref[...].astype(BF16)
    wu16_ref[...] = wu_ref[...].astype(BF16)
    wd16_ref[...] = wd_ref[...].astype(BF16)
    _ffn_chunk(h_ref[...], wg16_ref[...], wu16_ref[...], lambda cols: wd16_ref[:, cols], o_ref)

    if post_norm:
        @pl.when(pl.program_id(0) == pl.num_programs(0) - 1)
        def _():
            _norm_in_place(o_ref, post_g_ref, FFN_TM)


def _ffn_tail_kernel(x_ref, g_ref, post_g_ref, wg16_ref, wu16_ref, wd16_ref, o_ref, h_ref, *, post_norm):
    @pl.when(pl.program_id(1) == 0)
    def _():
        _norm_into(x_ref, g_ref, h_ref, FFN_TM, copy_ref=o_ref)

    for c in range(FFN_TAIL_CHUNKS):
        rows = slice(c * FFN_TF, (c + 1) * FFN_TF)
        _ffn_chunk(h_ref[...], wg16_ref[c], wu16_ref[c],
                   lambda cols, rows=rows: wd16_ref[rows, cols], o_ref)

    if post_norm:
        @pl.when(pl.program_id(1) == pl.num_programs(1) - 1)
        def _():
            _norm_in_place(o_ref, post_g_ref, FFN_TM)


def _ffn(x, g, wg, wu, wd, layer, post_g, post_norm=False):
    n_chunks = D_FF // FFN_TF
    out, wg16, wu16, wd16 = pl.pallas_call(
        functools.partial(_ffn_head_kernel, post_norm=post_norm),
        grid=(n_chunks,),
        in_specs=[
            pl.BlockSpec((FFN_TM, D_MODEL), lambda j: (0, 0)),
            pl.BlockSpec((None, 1, D_MODEL), lambda j: (layer, 0, 0)),
            pl.BlockSpec((1, D_MODEL), lambda j: (0, 0)),
            pl.BlockSpec((None, D_MODEL, FFN_TF), lambda j: (layer, 0, j)),
            pl.BlockSpec((None, D_MODEL, FFN_TF), lambda j: (layer, 0, j)),
            pl.BlockSpec((None, FFN_TF, D_MODEL), lambda j: (layer, j, 0)),
        ],
        out_specs=[
            pl.BlockSpec((FFN_TM, D_MODEL), lambda j: (0, 0)),
            pl.BlockSpec((None, D_MODEL, FFN_TF), lambda j: (j, 0, 0)),
            pl.BlockSpec((None, D_MODEL, FFN_TF), lambda j: (j, 0, 0)),
            pl.BlockSpec((FFN_TF, D_MODEL), lambda j: (j, 0)),
        ],
        out_shape=[
            jax.ShapeDtypeStruct((TOKENS, D_MODEL), F32),
            jax.ShapeDtypeStruct((n_chunks, D_MODEL, FFN_TF), BF16),
            jax.ShapeDtypeStruct((n_chunks, D_MODEL, FFN_TF), BF16),
            jax.ShapeDtypeStruct((D_FF, D_MODEL), BF16),
        ],
        input_output_aliases={0: 0},
        scratch_shapes=[pltpu.VMEM((FFN_TM, D_MODEL), BF16)],
        compiler_params=_params("arbitrary"),
        name="ffn_head",
    )(x, g, post_g, wg, wu, wd)
    return pl.pallas_call(
        functools.partial(_ffn_tail_kernel, post_norm=post_norm),
        grid=(TOKENS // FFN_TM - 1, n_chunks // FFN_TAIL_CHUNKS),
        in_specs=[
            pl.BlockSpec((FFN_TM, D_MODEL), lambda i, j: (i + 1, 0)),
            pl.BlockSpec((None, 1, D_MODEL), lambda i, j: (layer, 0, 0)),
            pl.BlockSpec((1, D_MODEL), lambda i, j: (0, 0)),
            pl.BlockSpec((FFN_TAIL_CHUNKS, D_MODEL, FFN_TF), lambda i, j: (j, 0, 0)),
            pl.BlockSpec((FFN_TAIL_CHUNKS, D_MODEL, FFN_TF), lambda i, j: (j, 0, 0)),
            pl.BlockSpec((FFN_TAIL_CHUNKS * FFN_TF, D_MODEL), lambda i, j: (j, 0)),
        ],
        out_specs=pl.BlockSpec((FFN_TM, D_MODEL), lambda i, j: (i + 1, 0)),
        out_shape=jax.ShapeDtypeStruct((TOKENS, D_MODEL), F32),
        input_output_aliases={0: 0},
        scratch_shapes=[pltpu.VMEM((FFN_TM, D_MODEL), BF16)],
        compiler_params=_params("parallel", "arbitrary"),
        name="ffn_tail",
    )(out, g, post_g, wg16, wu16, wd16)


PROJ_TN = 512
PREP_TM = 512


def _project_into(x_ref, g_ref, w_ref, z_ref, h_ref):
    _norm_into(x_ref, g_ref, h_ref, PREP_TM)
    nt = (((1,), (1,)), ((), ()))
    for c in range(Z_WIDTH // PROJ_TN):
        cols = slice(c * PROJ_TN, (c + 1) * PROJ_TN)
        z_ref[:, cols] = lax.dot_general(h_ref[...], w_ref[cols, :], nt, preferred_element_type=F32)


def _rope(x, cos, sin_signed):
    return x * cos + pltpu.roll(x, LANES // 2, axis=1) * sin_signed


def _prep_kernel(x_ref, g_ref, w_ref, cos_ref, sin_ref, cosm_ref, sinm_ref, qn_ref, kvn_ref,
                 wuq_ref, wukv_ref,
                 mlaq_ref, mlak_ref, mlav_ref, mq_ref, mk_ref, mv_ref,
                 sq_ref, sk_ref, sv_ref, z_ref, h_ref):
    _project_into(x_ref, g_ref, w_ref, z_ref, h_ref)
    cos = cos_ref[...]
    sin = sin_ref[...]
    cosm = cosm_ref[...]
    sinm = sinm_ref[...]
    mla_scale = (MLA_NOPE_DIM + MLA_ROPE_DIM) ** -0.5
    scale = HEAD_DIM ** -0.5

    cq = _rmsnorm_rows(z_ref[:, Z_CQ:Z_CQ + MLA_Q_LORA], qn_ref[...]).astype(BF16)
    q = jnp.dot(cq, wuq_ref[...], preferred_element_type=F32)
    for hd in range(N_MLA_HEADS):
        base = hd * MLA_QK_PAD
        mlaq_ref[:, base:base + LANES] = (q[:, base:base + LANES] * mla_scale).astype(BF16)
        pe = _rope(q[:, base + LANES:base + 2 * LANES], cosm, sinm)
        mlaq_ref[:, base + LANES:base + 2 * LANES] = (pe * mla_scale).astype(BF16)

    ones = jnp.ones((PREP_TM, LANES), BF16)
    ckv = _rmsnorm_rows(z_ref[:, Z_CKV:Z_CKV + MLA_KV_LORA], kvn_ref[...]).astype(BF16)
    kv = jnp.dot(ckv, wukv_ref[...], preferred_element_type=F32)
    kpe = _rope(z_ref[:, Z_KROPE:Z_KROPE + LANES], cosm, sinm).astype(BF16)
    for hd in range(N_MLA_HEADS):
        base = hd * MLA_QK_PAD
        mlak_ref[:, base:base + LANES] = kv[:, base:base + LANES].astype(BF16)
        mlak_ref[:, base + LANES:base + 2 * LANES] = kpe
        mlav_ref[:, base:base + LANES] = kv[:, base + LANES:base + 2 * LANES].astype(BF16)
        mlav_ref[:, base + LANES:base + 2 * LANES] = ones

    pos = (pl.program_id(0) % (SEQ // PREP_TM)) * PREP_TM + lax.broadcasted_iota(jnp.int32, (PREP_TM, LANES), 0)
    lane = lax.broadcasted_iota(jnp.int32, (PREP_TM, LANES), 1)
    block_onehot = jnp.where(pos // MOBA_BLOCK == lane, 1.0, 0.0).astype(BF16)
    for hd in range(N_MOBA_HEADS):
        c = hd * LANES
        mq_ref[:, c:c + LANES] = (_rope(z_ref[:, Z_MQ + c:Z_MQ + c + LANES], cos, sin) * scale).astype(BF16)
        mk_ref[:, 2 * c:2 * c + LANES] = _rope(z_ref[:, Z_MK + c:Z_MK + c + LANES], cos, sin).astype(BF16)
        mk_ref[:, 2 * c + LANES:2 * c + 2 * LANES] = block_onehot
        mv_ref[:, 2 * c:2 * c + LANES] = z_ref[:, Z_MV + c:Z_MV + c + LANES].astype(BF16)
        mv_ref[:, 2 * c + LANES:2 * c + 2 * LANES] = ones

    for hd in range(N_SWA_HEADS):
        c = hd * LANES
        sq_ref[:, c:c + LANES] = (_rope(z_ref[:, Z_SQ + c:Z_SQ + c + LANES], cos, sin) * scale).astype(BF16)
    for hd in range(N_SWA_KV_HEADS):
        c = hd * LANES
        sk_ref[:, c:c + LANES] = _rope(z_ref[:, Z_SK + c:Z_SK + c + LANES], cos, sin).astype(BF16)
        sv_ref[:, 2 * c:2 * c + LANES] = z_ref[:, Z_SV + c:Z_SV + c + LANES].astype(BF16)
        sv_ref[:, 2 * c + LANES:2 * c + 2 * LANES] = ones


def _prep(x, g, w_in, tabs, qn, kvn, wuq, wukv, layer):
    nblk = SEQ // PREP_TM
    row = lambda i: (i, 0)
    tab = lambda i: (i % nblk, 0)
    const = lambda i: (0, 0)
    of_layer = lambda i: (layer, 0, 0)
    widths = [N_MLA_HEADS * MLA_QK_PAD, N_MLA_HEADS * MLA_QK_PAD, N_MLA_HEADS * 2 * MLA_V_DIM,
              N_MOBA_HEADS * HEAD_DIM, N_MOBA_HEADS * MOBA_QK_AUG, N_MOBA_HEADS * 2 * HEAD_DIM,
              N_SWA_HEADS * HEAD_DIM, N_SWA_KV_HEADS * HEAD_DIM, N_SWA_KV_HEADS * 2 * HEAD_DIM]
    return pl.pallas_call(
        _prep_kernel,
        grid=(TOKENS // PREP_TM,),
        in_specs=[
            pl.BlockSpec((PREP_TM, D_MODEL), row),
            pl.BlockSpec((1, D_MODEL), const),
            pl.BlockSpec((None, Z_WIDTH, D_MODEL), of_layer, pipeline_mode=pl.Buffered(1)),
            pl.BlockSpec((PREP_TM, LANES), tab),
            pl.BlockSpec((PREP_TM, LANES), tab),
            pl.BlockSpec((PREP_TM, LANES), tab),
            pl.BlockSpec((PREP_TM, LANES), tab),
            pl.BlockSpec((1, MLA_Q_LORA), const),
            pl.BlockSpec((1, MLA_KV_LORA), const),
            pl.BlockSpec((None, MLA_Q_LORA, N_MLA_HEADS * MLA_QK_PAD), of_layer),
            pl.BlockSpec((None, MLA_KV_LORA, N_MLA_HEADS * MLA_QK_PAD), of_layer),
        ],
        out_specs=[pl.BlockSpec((PREP_TM, w), row) for w in widths],
        out_shape=[jax.ShapeDtypeStruct((TOKENS, w), BF16) for w in widths],
        scratch_shapes=[pltpu.VMEM((PREP_TM, Z_WIDTH), F32), pltpu.VMEM((PREP_TM, D_MODEL), BF16)],
        compiler_params=_params("parallel"),
        name="mixer_prep",
    )(x, g.reshape(1, D_MODEL), w_in, *tabs, qn.reshape(1, -1), kvn.reshape(1, -1), wuq, wukv)


ATT_TQ = MOBA_BLOCK
ATT_TK = 2 * MOBA_BLOCK


def _split_bf16(x):
    hi = x.astype(BF16)
    lo = (x - hi.astype(F32)).astype(BF16)
    return hi, lo


MASKED = -1e30


def _moba_gate_logits(q, kbar, qi):
    nblk = SEQ // MOBA_BLOCK
    nt = (((1,), (1,)), ((), ()))
    kb_hi, kb_lo = _split_bf16(kbar)
    gate = (lax.dot_general(kb_hi, q, nt, preferred_element_type=F32)
            + lax.dot_general(kb_lo, q, nt, preferred_element_type=F32))[:nblk]
    blk = lax.broadcasted_iota(jnp.int32, gate.shape, 0)
    gate = jnp.where(blk < qi, gate, NEG_INF)
    rank = jnp.zeros(gate.shape, jnp.int32)
    for other in range(nblk):
        row = gate[other:other + 1, :]
        beats = (row > gate) | ((row == gate) & (other < blk))
        rank = rank + jnp.where(beats, 1, 0)
    keep = ((rank < MOBA_TOPK) & (blk < qi)) | (blk == qi)
    logit_t = jnp.where(keep, 0.0, MASKED)
    logit_t = jnp.concatenate([logit_t, jnp.full((LANES - nblk, gate.shape[1]), MASKED, F32)], axis=0)
    return logit_t.T


def _attn_kernel(q_ref, k_ref, v_ref, o_ref, m_ref, mnext_ref, acc_ref, s_ref, *moba_scratch,
                 n_heads, dq, dv, moba):
    qi = pl.program_id(1)
    nt = (((1,), (1,)), ((), ()))

    if moba:
        kbar_ref, qaug_ref = moba_scratch

        @pl.when(qi == 0)
        def _():
            kbar_ref[...] = jnp.zeros_like(kbar_ref)

            def body(b, carry):
                sl = pl.ds(pl.multiple_of(b * MOBA_BLOCK, MOBA_BLOCK), MOBA_BLOCK)
                kbar_ref[pl.ds(b, 1), :] = jnp.mean(k_ref[sl, :].astype(F32), axis=0, keepdims=True)
                return carry

            lax.fori_loop(0, SEQ // MOBA_BLOCK, body, 0)

        for hd in range(n_heads):
            qh = q_ref[:, hd * HEAD_DIM:(hd + 1) * HEAD_DIM]
            logits = _moba_gate_logits(qh, kbar_ref[:, hd * dq:hd * dq + HEAD_DIM], qi)
            qaug_ref[:, hd * dq:hd * dq + HEAD_DIM] = qh
            qaug_ref[:, hd * dq + HEAD_DIM:(hd + 1) * dq] = logits.astype(BF16)
        q_src = qaug_ref
    else:
        q_src = q_ref

    def tile(t):
        return pl.ds(pl.multiple_of(t * ATT_TK, ATT_TK), ATT_TK)

    r = lax.broadcasted_iota(jnp.int32, (ATT_TQ, ATT_TQ), 0)
    c = lax.broadcasted_iota(jnp.int32, (ATT_TQ, ATT_TQ), 1)
    tri = c <= r
    if ATT_TK == ATT_TQ:
        t_diag = qi
        mask = tri
    else:
        t_diag = qi // 2
        odd = (qi % 2) == 1
        mask = jnp.concatenate([tri | odd, tri & odd], axis=1)
    dva = 2 * dv

    def scores(t, hd):
        return lax.dot_general(q_src[:, hd * dq:(hd + 1) * dq], k_ref[tile(t), hd * dq:(hd + 1) * dq], nt,
                               preferred_element_type=F32)

    def weighted_values(p, t, hd):
        return jnp.dot(p.astype(BF16), v_ref[tile(t), hd * dva:(hd + 1) * dva], preferred_element_type=F32)

    heads = range(n_heads)
    for hd in heads:
        s = jnp.where(mask, scores(t_diag, hd), NEG_INF)
        m = jnp.max(s, axis=-1, keepdims=True)
        m_ref[hd] = m
        acc_ref[hd] = weighted_values(jnp.exp(s - m), t_diag, hd)

    def body(t, carry):
        t_next = jnp.minimum(t + 1, t_diag - 1)
        for hd in heads:
            m_cur = mnext_ref[hd]
            p = jnp.exp(s_ref[hd] - m_cur)
            s = scores(t_next, hd)
            s_ref[hd] = s
            mnext_ref[hd] = jnp.maximum(m_cur, jnp.max(s, axis=-1, keepdims=True))
            alpha = jnp.exp(m_ref[hd] - m_cur)
            acc_ref[hd] = alpha * acc_ref[hd] + weighted_values(p, t, hd)
            m_ref[hd] = m_cur
        return carry

    for hd in heads:
        s = scores(0, hd)
        s_ref[hd] = s
        mnext_ref[hd] = jnp.maximum(m_ref[hd], jnp.max(s, axis=-1, keepdims=True))
    lax.fori_loop(0, t_diag, body, 0)
    for hd in heads:
        acc = acc_ref[hd]
        o_ref[:, hd * dv:(hd + 1) * dv] = (acc[:, :dv] / acc[:, dv:]).astype(o_ref.dtype)


def _attention(q, k, v, n_heads, dq, dv, moba):
    nq = SEQ // ATT_TQ
    scratch = [pltpu.VMEM((n_heads, ATT_TQ, 1), F32), pltpu.VMEM((n_heads, ATT_TQ, 1), F32),
               pltpu.VMEM((n_heads, ATT_TQ, 2 * dv), F32), pltpu.VMEM((n_heads, ATT_TQ, ATT_TK), F32)]
    if moba:
        scratch += [pltpu.VMEM((LANES, n_heads * dq), F32), pltpu.VMEM((ATT_TQ, n_heads * dq), BF16)]
    return pl.pallas_call(
        functools.partial(_attn_kernel, n_heads=n_heads, dq=dq, dv=dv, moba=moba),
        grid=(BATCH, nq),
        in_specs=[
            pl.BlockSpec((ATT_TQ, q.shape[1]), lambda b, i: (b * nq + i, 0)),
            pl.BlockSpec((SEQ, n_heads * dq), lambda b, i: (b, 0)),
            pl.BlockSpec((SEQ, n_heads * 2 * dv), lambda b, i: (b, 0)),
        ],
        out_specs=pl.BlockSpec((ATT_TQ, n_heads * dv), lambda b, i: (b * nq + i, 0)),
        out_shape=jax.ShapeDtypeStruct((TOKENS, n_heads * dv), BF16),
        scratch_shapes=scratch,
        compiler_params=_params("parallel", "arbitrary"),
        name="moba_attn" if moba else "mla_attn",
    )(q, k, v)


SWA_STEP_BLOCKS = 4


OUT_TN = 512
W_MLA = N_MLA_HEADS * MLA_V_DIM
W_MOBA = N_MOBA_HEADS * HEAD_DIM
W_SWA = N_SWA_HEADS * HEAD_DIM
MIX_WIDTH = W_MLA + W_MOBA + W_SWA
SWA_STEP_ROWS = SWA_STEP_BLOCKS * SWA_WINDOW


def _swa_out_kernel(sink_ref, x_ref, a_ref, b_ref, q_ref, kp_ref, kc_ref, vp_ref, vc_ref, w_ref, o_ref,
                    w16_ref, s_ref, p_ref, m_ref, c_ref):
    step = pl.program_id(0)
    n = step % (SEQ // SWA_STEP_ROWS)
    W = SWA_WINDOW
    G = SWA_GROUP
    nt = (((1,), (1,)), ((), ()))
    blocks = range(SWA_STEP_BLOCKS)
    col_chunks = [slice(c * OUT_TN, (c + 1) * OUT_TN) for c in range(D_MODEL // OUT_TN)]

    @pl.when(step == 0)
    def _():
        def body(r, carry):
            sl = pl.ds(pl.multiple_of(r * NORM_ROWS, NORM_ROWS), NORM_ROWS)
            w16_ref[sl, :] = w_ref[sl, :].astype(BF16)
            return carry

        lax.fori_loop(0, MIX_WIDTH // NORM_ROWS, body, 0)

    r = lax.broadcasted_iota(jnp.int32, (G * W, 2 * W), 0) % W
    c = lax.broadcasted_iota(jnp.int32, (G * W, 2 * W), 1)
    rel = r + W - c
    band = (rel >= 0) & (rel < W)

    def project(lhs, w_rows, cols, first):
        part = jnp.dot(lhs, w16_ref[w_rows, cols], preferred_element_type=F32)
        if first:
            o_ref[:, cols] = x_ref[:, cols] + part
        else:
            o_ref[:, cols] += part

    fillers = [
        [functools.partial(project, a_ref[...], slice(0, W_MLA), cols, True) for cols in col_chunks],
        [functools.partial(project, b_ref[...], slice(W_MLA, W_MLA + W_MOBA), cols, False) for cols in col_chunks],
    ]

    for hk in range(N_SWA_KV_HEADS):
        kcols = slice(hk * HEAD_DIM, (hk + 1) * HEAD_DIM)
        vcols = slice(hk * 2 * HEAD_DIM, (hk + 1) * 2 * HEAD_DIM)
        kk = jnp.concatenate([kp_ref[:, kcols], kc_ref[:, kcols]], axis=0)
        vv = jnp.concatenate([vp_ref[:, vcols], vc_ref[:, vcols]], axis=0)
        sink = jnp.concatenate([jnp.full((W, 1), sink_ref[hk * G + g], F32) for g in range(G)], axis=0)
        fill = fillers[hk]
        for blk in blocks:
            q = jnp.concatenate([q_ref[blk * W:(blk + 1) * W, (hk * G + g) * LANES:(hk * G + g + 1) * LANES]
                                 for g in range(G)], axis=0)
            s = lax.dot_general(q, kk[blk * W:(blk + 2) * W], nt, preferred_element_type=F32)
            mask = band & ((c >= W) | (n > 0)) if blk == 0 else band
            s_ref[blk] = jnp.where(mask, s, NEG_INF)
        fill[0]()
        for blk in blocks:
            m_ref[blk] = jnp.maximum(jnp.max(s_ref[blk], axis=-1, keepdims=True), sink)
        fill[1]()
        for blk in blocks:
            p_ref[blk] = jnp.exp(s_ref[blk] - m_ref[blk]).astype(BF16)
        fill[2]()
        for blk in blocks:
            pv = jnp.dot(p_ref[blk], vv[blk * W:(blk + 2) * W], preferred_element_type=F32)
            o = pv[:, :LANES] / (pv[:, LANES:] + jnp.exp(sink - m_ref[blk]))
            for g in range(G):
                c_ref[blk * W:(blk + 1) * W, (hk * G + g) * LANES:(hk * G + g + 1) * LANES] = (
                    o[g * W:(g + 1) * W].astype(BF16))
        fill[3]()

    for cols in col_chunks:
        project(c_ref[...], slice(W_MLA + W_MOBA, MIX_WIDTH), cols, False)


def _swa_out(sinks, x, a, b, q, k, v, w, layer):
    W = SWA_WINDOW
    nsteps = SEQ // SWA_STEP_ROWS
    cur = lambda i: (i, 0)
    prev = lambda i: ((i // nsteps) * (SEQ // W) + jnp.maximum((i % nsteps) * SWA_STEP_BLOCKS - 1, 0), 0)
    kw = N_SWA_KV_HEADS * HEAD_DIM
    return pl.pallas_call(
        _swa_out_kernel,
        grid=(TOKENS // SWA_STEP_ROWS,),
        in_specs=[
            pl.BlockSpec(memory_space=pltpu.SMEM),
            pl.BlockSpec((SWA_STEP_ROWS, D_MODEL), cur),
            pl.BlockSpec((SWA_STEP_ROWS, W_MLA), cur),
            pl.BlockSpec((SWA_STEP_ROWS, W_MOBA), cur),
            pl.BlockSpec((SWA_STEP_ROWS, W_SWA), cur),
            pl.BlockSpec((W, kw), prev),
            pl.BlockSpec((SWA_STEP_ROWS, kw), cur),
            pl.BlockSpec((W, 2 * kw), prev),
            pl.BlockSpec((SWA_STEP_ROWS, 2 * kw), cur),
            pl.BlockSpec((None, MIX_WIDTH, D_MODEL), lambda i: (layer, 0, 0), pipeline_mode=pl.Buffered(1)),
        ],
        out_specs=pl.BlockSpec((SWA_STEP_ROWS, D_MODEL), cur),
        out_shape=jax.ShapeDtypeStruct((TOKENS, D_MODEL), F32),
        scratch_shapes=[pltpu.VMEM((MIX_WIDTH, D_MODEL), BF16),
                        pltpu.VMEM((SWA_STEP_BLOCKS, SWA_GROUP * W, 2 * W), F32),
                        pltpu.VMEM((SWA_STEP_BLOCKS, SWA_GROUP * W, 2 * W), BF16),
                        pltpu.VMEM((SWA_STEP_BLOCKS, SWA_GROUP * W, 1), F32),
                        pltpu.VMEM((SWA_STEP_ROWS, W_SWA), BF16)],
        compiler_params=_params("arbitrary"),
        name="swa_out",
    )(sinks, x, a, b, q, k, k, v, v, w)


def _rope_tables():
    def angles(d):
        half = d // 2
        inv_freq = 1.0 / (ROPE_THETA ** (jnp.arange(half, dtype=F32) * (2.0 / d)))
        return jnp.arange(SEQ, dtype=F32)[:, None] * inv_freq[None, :]

    ang = angles(HEAD_DIM)
    cos = jnp.concatenate([jnp.cos(ang), jnp.cos(ang)], axis=1)
    sin = jnp.concatenate([-jnp.sin(ang), jnp.sin(ang)], axis=1)
    angm = angles(MLA_ROPE_DIM)
    zero = jnp.zeros_like(angm)
    cosm = jnp.concatenate([jnp.cos(angm), zero, jnp.cos(angm), zero], axis=1)
    sinm = jnp.concatenate([-jnp.sin(angm), zero, jnp.sin(angm), zero], axis=1)
    return cos, sin, cosm, sinm


def _spread_rope_cols(w):
    half = MLA_ROPE_DIM // 2
    zero = jnp.zeros(w.shape[:-1] + (half,), w.dtype)
    return jnp.concatenate([w[..., :half], zero, w[..., half:], zero], axis=-1)


LAYOUT_COLS = 512


def _layout_w_in_kernel(w_ref, o_ref):
    half = MLA_ROPE_DIM // 2
    src = Z_KROPE + 2 * half
    o_ref[0:Z_KROPE + half, :] = w_ref[0:Z_KROPE + half, :].astype(BF16)
    o_ref[Z_KROPE + half:Z_KROPE + 2 * half, :] = jnp.zeros((half, LAYOUT_COLS), BF16)
    o_ref[Z_KROPE + 2 * half:Z_KROPE + 3 * half, :] = w_ref[Z_KROPE + half:src, :].astype(BF16)
    o_ref[Z_KROPE + 3 * half:Z_MQ, :] = jnp.zeros((half, LAYOUT_COLS), BF16)
    n_rest = w_ref.shape[0] - src
    o_ref[Z_MQ:Z_MQ + n_rest, :] = w_ref[src:, :].astype(BF16)
    o_ref[Z_MQ + n_rest:, :] = jnp.zeros((Z_WIDTH - Z_MQ - n_rest, LAYOUT_COLS), BF16)


def _layout_w_in(w):
    wt = jnp.swapaxes(w, 1, 2)
    return pl.pallas_call(
        _layout_w_in_kernel,
        grid=(DEPTH, D_MODEL // LAYOUT_COLS),
        in_specs=[pl.BlockSpec((None, wt.shape[1], LAYOUT_COLS), lambda l, i: (l, 0, i))],
        out_specs=pl.BlockSpec((None, Z_WIDTH, LAYOUT_COLS), lambda l, i: (l, 0, i)),
        out_shape=jax.ShapeDtypeStruct((DEPTH, Z_WIDTH, D_MODEL), BF16),
        compiler_params=_params("parallel", "parallel"),
        name="layout_w_in",
    )(wt)


def _layout_w_uq(w):
    w = w.astype(BF16).reshape(DEPTH, MLA_Q_LORA, N_MLA_HEADS, MLA_NOPE_DIM + MLA_ROPE_DIM)
    w = jnp.concatenate([w[..., :MLA_NOPE_DIM], _spread_rope_cols(w[..., MLA_NOPE_DIM:])], axis=-1)
    return w.reshape(DEPTH, MLA_Q_LORA, N_MLA_HEADS * MLA_QK_PAD)


def kernel(x, ffn1_norm, ffn1_w_gate, ffn1_w_up, ffn1_w_down, attn_norm, w_in, mla_q_norm, mla_w_uq, mla_kv_norm, mla_w_ukv, swa_sinks, w_out, ffn2_norm, ffn2_w_gate, ffn2_w_up, ffn2_w_down, final_norm):
    tabs = _rope_tables()
    x = x.reshape(TOKENS, D_MODEL)
    ffn1_g = ffn1_norm.reshape(DEPTH, 1, D_MODEL)
    ffn2_g = ffn2_norm.reshape(DEPTH, 1, D_MODEL)
    w_in16 = _layout_w_in(w_in)
    w_uq16 = _layout_w_uq(mla_w_uq)
    w_ukv16 = mla_w_ukv.astype(BF16)
    final_g = final_norm.reshape(1, D_MODEL)
    for l in range(DEPTH):
        x = _ffn(x, ffn1_g, ffn1_w_gate, ffn1_w_up, ffn1_w_down, l, final_g)
        (mla_q, mla_k, mla_v, mq, mk, mv, sq, sk, sv) = _prep(
            x, attn_norm[l], w_in16, tabs, mla_q_norm[l], mla_kv_norm[l], w_uq16, w_ukv16, l)
        o_mla = _attention(mla_q, mla_k, mla_v, N_MLA_HEADS, MLA_QK_PAD, MLA_V_DIM, moba=False)
        o_moba = _attention(mq, mk, mv, N_MOBA_HEADS, MOBA_QK_AUG, HEAD_DIM, moba=True)
        x = _swa_out(swa_sinks[l], x, o_mla, o_moba, sq, sk, sv, w_out, l)
        x = _ffn(x, ffn2_g, ffn2_w_gate, ffn2_w_up, ffn2_w_down, l, final_g, post_norm=(l == DEPTH - 1))
    return x.reshape(BATCH, SEQ, D_MODEL)
```

```python
import functools

import jax
import jax.numpy as jnp
import numpy as np
from jax import lax
from jax.experimental import pallas as pl
from jax.experimental.pallas import tpu as pltpu

D_MODEL = 2048
BATCH = 2
SEQ = 4096
DEPTH = 2
TOKENS = BATCH * SEQ

HEAD_DIM = 128
N_MLA_HEADS = 4
MLA_Q_LORA = 512
MLA_KV_LORA = 256
MLA_NOPE_DIM = 128
MLA_ROPE_DIM = 64
MLA_V_DIM = 128
MLA_QK_PAD = 256
N_MOBA_HEADS = 4
MOBA_BLOCK = 256
MOBA_TOPK = 3
MOBA_QK_AUG = 256
N_SWA_HEADS = 8
N_SWA_KV_HEADS = 2
SWA_GROUP = N_SWA_HEADS // N_SWA_KV_HEADS
SWA_WINDOW = 128
D_FF = 5632
ROPE_THETA = 10000.0
NORM_EPS = 1e-6

LANES = 128
Z_WIDTH = 4096

Z_CQ = 0
Z_CKV = 512
Z_KROPE = 768
Z_MQ = 896
Z_MK = 1408
Z_MV = 1920
Z_SQ = 2432
Z_SK = 3456
Z_SV = 3712

VMEM_LIMIT = 56 * 1024 * 1024

BF16 = jnp.bfloat16
F32 = jnp.float32
NEG_INF = float("-inf")


def _params(*sem, flags=None):
    return pltpu.CompilerParams(dimension_semantics=sem, vmem_limit_bytes=VMEM_LIMIT, flags=flags)


FFN_TM = 1024
FFN_TF = 256
FFN_TAIL_CHUNKS = 2
FFN_DOWN_TN = 512
NORM_ROWS = 128


def _rmsnorm_rows(x, g):
    ms = jnp.mean(x * x, axis=-1, keepdims=True)
    return x * lax.rsqrt(ms + NORM_EPS) * g


def _norm_into(x_ref, g_ref, h_ref, rows, copy_ref=None):
    g = g_ref[...]

    def body(r, carry):
        sl = pl.ds(pl.multiple_of(r * NORM_ROWS, NORM_ROWS), NORM_ROWS)
        x = x_ref[sl, :]
        h_ref[sl, :] = _rmsnorm_rows(x, g).astype(BF16)
        if copy_ref is not None:
            copy_ref[sl, :] = x
        return carry

    lax.fori_loop(0, rows // NORM_ROWS, body, 0)


def _ffn_chunk(h, wg, wu, wd_cols, o_ref):
    gate = jnp.dot(h, wg, preferred_element_type=F32)
    up = jnp.dot(h, wu, preferred_element_type=F32)
    act = (gate * jax.nn.sigmoid(gate) * (0.5 * up)).astype(BF16)
    for c in range(D_MODEL // FFN_DOWN_TN):
        cols = slice(c * FFN_DOWN_TN, (c + 1) * FFN_DOWN_TN)
        o_ref[:, cols] += jnp.dot(act, wd_cols(cols), preferred_element_type=F32)


def _norm_in_place(o_ref, g_ref, rows):
    g = g_ref[...]

    def body(r, carry):
        sl = pl.ds(pl.multiple_of(r * NORM_ROWS, NORM_ROWS), NORM_ROWS)
        o_ref[sl, :] = _rmsnorm_rows(o_ref[sl, :], g)
        return carry

    lax.fori_loop(0, rows // NORM_ROWS, body, 0)


def _ffn_head_kernel(x_ref, g_ref, post_g_ref, wg_ref, wu_ref, wd_ref, o_ref, wg16_ref, wu16_ref, wd16_ref, h_ref,
                     *, post_norm):
    @pl.when(pl.program_id(0) == 0)
    def _():
        _norm_into(x_ref, g_ref, h_ref, FFN_TM, copy_ref=o_ref)

    wg16_ref[...] = wg_ref[...].astype(BF16)
    wu16_ref[...] = wu_ref[...].astype(BF16)
    wd16_ref[...] = wd_ref[...].astype(BF16)
    _ffn_chunk(h_ref[...], wg16_ref[...], wu16_ref[...], lambda cols: wd16_ref[:, cols], o_ref)

    if post_norm:
        @pl.when(pl.program_id(0) == pl.num_programs(0) - 1)
        def _():
            _norm_in_place(o_ref, post_g_ref, FFN_TM)


def _ffn_tail_kernel(x_ref, g_ref, post_g_ref, wg16_ref, wu16_ref, wd16_ref, o_ref, h_ref, *, post_norm):
    @pl.when(pl.program_id(1) == 0)
    def _():
        _norm_into(x_ref, g_ref, h_ref, FFN_TM, copy_ref=o_ref)

    for c in range(FFN_TAIL_CHUNKS):
        rows = slice(c * FFN_TF, (c + 1) * FFN_TF)
        _ffn_chunk(h_ref[...], wg16_ref[c], wu16_ref[c],
                   lambda cols, rows=rows: wd16_ref[rows, cols], o_ref)

    if post_norm:
        @pl.when(pl.program_id(1) == pl.num_programs(1) - 1)
        def _():
            _norm_in_place(o_ref, post_g_ref, FFN_TM)


def _ffn(x, g, wg, wu, wd, layer, post_g, post_norm=False):
    n_chunks = D_FF // FFN_TF
    out, wg16, wu16, wd16 = pl.pallas_call(
        functools.partial(_ffn_head_kernel, post_norm=post_norm),
        grid=(n_chunks,),
        in_specs=[
            pl.BlockSpec((FFN_TM, D_MODEL), lambda j: (0, 0)),
            pl.BlockSpec((None, 1, D_MODEL), lambda j: (layer, 0, 0)),
            pl.BlockSpec((1, D_MODEL), lambda j: (0, 0)),
            pl.BlockSpec((None, D_MODEL, FFN_TF), lambda j: (layer, 0, j)),
            pl.BlockSpec((None, D_MODEL, FFN_TF), lambda j: (layer, 0, j)),
            pl.BlockSpec((None, FFN_TF, D_MODEL), lambda j: (layer, j, 0)),
        ],
        out_specs=[
            pl.BlockSpec((FFN_TM, D_MODEL), lambda j: (0, 0)),
            pl.BlockSpec((None, D_MODEL, FFN_TF), lambda j: (j, 0, 0)),
            pl.BlockSpec((None, D_MODEL, FFN_TF), lambda j: (j, 0, 0)),
            pl.BlockSpec((FFN_TF, D_MODEL), lambda j: (j, 0)),
        ],
        out_shape=[
            jax.ShapeDtypeStruct((TOKENS, D_MODEL), F32),
            jax.ShapeDtypeStruct((n_chunks, D_MODEL, FFN_TF), BF16),
            jax.ShapeDtypeStruct((n_chunks, D_MODEL, FFN_TF), BF16),
            jax.ShapeDtypeStruct((D_FF, D_MODEL), BF16),
        ],
        input_output_aliases={0: 0},
        scratch_shapes=[pltpu.VMEM((FFN_TM, D_MODEL), BF16)],
        compiler_params=_params("arbitrary"),
        name="ffn_head",
    )(x, g, post_g, wg, wu, wd)
    return pl.pallas_call(
        functools.partial(_ffn_tail_kernel, post_norm=post_norm),
        grid=(TOKENS // FFN_TM - 1, n_chunks // FFN_TAIL_CHUNKS),
        in_specs=[
            pl.BlockSpec((FFN_TM, D_MODEL), lambda i, j: (i + 1, 0)),
            pl.BlockSpec((None, 1, D_MODEL), lambda i, j: (layer, 0, 0)),
            pl.BlockSpec((1, D_MODEL), lambda i, j: (0, 0)),
            pl.BlockSpec((FFN_TAIL_CHUNKS, D_MODEL, FFN_TF), lambda i, j: (j, 0, 0)),
            pl.BlockSpec((FFN_TAIL_CHUNKS, D_MODEL, FFN_TF), lambda i, j: (j, 0, 0)),
            pl.BlockSpec((FFN_TAIL_CHUNKS * FFN_TF, D_MODEL), lambda i, j: (j, 0)),
        ],
        out_specs=pl.BlockSpec((FFN_TM, D_MODEL), lambda i, j: (i + 1, 0)),
        out_shape=jax.ShapeDtypeStruct((TOKENS, D_MODEL), F32),
        input_output_aliases={0: 0},
        scratch_shapes=[pltpu.VMEM((FFN_TM, D_MODEL), BF16)],
        compiler_params=_params("parallel", "arbitrary"),
        name="ffn_tail",
    )(out, g, post_g, wg16, wu16, wd16)


PROJ_TN = 512
PREP_TM = 512


def _project_into(x_ref, g_ref, w_ref, z_ref, h_ref):
    _norm_into(x_ref, g_ref, h_ref, PREP_TM)
    nt = (((1,), (1,)), ((), ()))
    for c in range(Z_WIDTH // PROJ_TN):
        cols = slice(c * PROJ_TN, (c + 1) * PROJ_TN)
        z_ref[:, cols] = lax.dot_general(h_ref[...], w_ref[cols, :], nt, preferred_element_type=F32)


def _rope(x, cos, sin_signed):
    return x * cos + pltpu.roll(x, LANES // 2, axis=1) * sin_signed


P_MLA_Q = 0
P_MLA_K = P_MLA_Q + N_MLA_HEADS * MLA_QK_PAD
P_MLA_V = P_MLA_K + N_MLA_HEADS * MLA_QK_PAD
P_MOBA_K = P_MLA_V + N_MLA_HEADS * 2 * MLA_V_DIM
P_MOBA_V = P_MOBA_K + N_MOBA_HEADS * MOBA_QK_AUG
P_SWA_Q = P_MOBA_V + N_MOBA_HEADS * 2 * HEAD_DIM
P_MOBA_Q = P_SWA_Q + N_SWA_HEADS * HEAD_DIM
P_SWA_V = P_MOBA_Q + N_MOBA_HEADS * HEAD_DIM
P_SWA_K = P_SWA_V + N_SWA_KV_HEADS * 2 * HEAD_DIM
P_WIDTH = P_SWA_K + N_SWA_KV_HEADS * HEAD_DIM


def _prep_kernel(x_ref, g_ref, w_ref, tab_ref, qn_ref, kvn_ref, wuq_ref, wukv_ref, p_ref, z_ref, h_ref):
    def section(offset, width):
        return p_ref.at[:, offset:offset + width]

    mlaq_ref = section(P_MLA_Q, N_MLA_HEADS * MLA_QK_PAD)
    mlak_ref = section(P_MLA_K, N_MLA_HEADS * MLA_QK_PAD)
    mlav_ref = section(P_MLA_V, N_MLA_HEADS * 2 * MLA_V_DIM)
    mq_ref = section(P_MOBA_Q, N_MOBA_HEADS * HEAD_DIM)
    mk_ref = section(P_MOBA_K, N_MOBA_HEADS * MOBA_QK_AUG)
    mv_ref = section(P_MOBA_V, N_MOBA_HEADS * 2 * HEAD_DIM)
    sq_ref = section(P_SWA_Q, N_SWA_HEADS * HEAD_DIM)
    sk_ref = section(P_SWA_K, N_SWA_KV_HEADS * HEAD_DIM)
    sv_ref = section(P_SWA_V, N_SWA_KV_HEADS * 2 * HEAD_DIM)
    _project_into(x_ref, g_ref, w_ref, z_ref, h_ref)
    cos, sin, cosm, sinm = (tab_ref[:, t * LANES:(t + 1) * LANES] for t in range(4))
    mla_scale = (MLA_NOPE_DIM + MLA_ROPE_DIM) ** -0.5
    scale = HEAD_DIM ** -0.5

    cq = _rmsnorm_rows(z_ref[:, Z_CQ:Z_CQ + MLA_Q_LORA], qn_ref[...]).astype(BF16)
    q = jnp.dot(cq, wuq_ref[...], preferred_element_type=F32)
    for hd in range(N_MLA_HEADS):
        base = hd * MLA_QK_PAD
        mlaq_ref[:, base:base + LANES] = (q[:, base:base + LANES] * mla_scale).astype(BF16)
        pe = _rope(q[:, base + LANES:base + 2 * LANES], cosm, sinm)
        mlaq_ref[:, base + LANES:base + 2 * LANES] = (pe * mla_scale).astype(BF16)

    ones = jnp.ones((PREP_TM, LANES), BF16)
    ckv = _rmsnorm_rows(z_ref[:, Z_CKV:Z_CKV + MLA_KV_LORA], kvn_ref[...]).astype(BF16)
    kv = jnp.dot(ckv, wukv_ref[...], preferred_element_type=F32)
    kpe = _rope(z_ref[:, Z_KROPE:Z_KROPE + LANES], cosm, sinm).astype(BF16)
    for hd in range(N_MLA_HEADS):
        base = hd * MLA_QK_PAD
        mlak_ref[:, base:base + LANES] = kv[:, base:base + LANES].astype(BF16)
        mlak_ref[:, base + LANES:base + 2 * LANES] = kpe
        mlav_ref[:, base:base + LANES] = kv[:, base + LANES:base + 2 * LANES].astype(BF16)
        mlav_ref[:, base + LANES:base + 2 * LANES] = ones

    pos = (pl.program_id(0) % (SEQ // PREP_TM)) * PREP_TM + lax.broadcasted_iota(jnp.int32, (PREP_TM, LANES), 0)
    lane = lax.broadcasted_iota(jnp.int32, (PREP_TM, LANES), 1)
    block_onehot = jnp.where(pos // MOBA_BLOCK == lane, 1.0, 0.0).astype(BF16)
    for hd in range(N_MOBA_HEADS):
        c = hd * LANES
        mq_ref[:, c:c + LANES] = (_rope(z_ref[:, Z_MQ + c:Z_MQ + c + LANES], cos, sin) * scale).astype(BF16)
        mk_ref[:, 2 * c:2 * c + LANES] = _rope(z_ref[:, Z_MK + c:Z_MK + c + LANES], cos, sin).astype(BF16)
        mk_ref[:, 2 * c + LANES:2 * c + 2 * LANES] = block_onehot
        mv_ref[:, 2 * c:2 * c + LANES] = z_ref[:, Z_MV + c:Z_MV + c + LANES].astype(BF16)
        mv_ref[:, 2 * c + LANES:2 * c + 2 * LANES] = ones

    for hd in range(N_SWA_HEADS):
        c = hd * LANES
        sq_ref[:, c:c + LANES] = (_rope(z_ref[:, Z_SQ + c:Z_SQ + c + LANES], cos, sin) * scale).astype(BF16)
    for hd in range(N_SWA_KV_HEADS):
        c = hd * LANES
        sk_ref[:, c:c + LANES] = _rope(z_ref[:, Z_SK + c:Z_SK + c + LANES], cos, sin).astype(BF16)
        sv_ref[:, 2 * c:2 * c + LANES] = z_ref[:, Z_SV + c:Z_SV + c + LANES].astype(BF16)
        sv_ref[:, 2 * c + LANES:2 * c + 2 * LANES] = ones


def _prep(x, g, w_in, tabs, qn, kvn, wuq, wukv, layer):
    nblk = SEQ // PREP_TM
    row = lambda i: (i, 0)
    tab = lambda i: (i % nblk, 0)
    const = lambda i: (0, 0)
    of_layer = lambda i: (layer, 0, 0)
    return pl.pallas_call(
        _prep_kernel,
        grid=(TOKENS // PREP_TM,),
        in_specs=[
            pl.BlockSpec((PREP_TM, D_MODEL), row),
            pl.BlockSpec((1, D_MODEL), const),
            pl.BlockSpec((None, Z_WIDTH, D_MODEL), of_layer, pipeline_mode=pl.Buffered(1)),
            pl.BlockSpec((PREP_TM, 4 * LANES), tab),
            pl.BlockSpec((1, MLA_Q_LORA), const),
            pl.BlockSpec((1, MLA_KV_LORA), const),
            pl.BlockSpec((None, MLA_Q_LORA, N_MLA_HEADS * MLA_QK_PAD), of_layer),
            pl.BlockSpec((None, MLA_KV_LORA, N_MLA_HEADS * MLA_QK_PAD), of_layer),
        ],
        out_specs=pl.BlockSpec((PREP_TM, P_WIDTH), row),
        out_shape=jax.ShapeDtypeStruct((TOKENS, P_WIDTH), BF16),
        scratch_shapes=[pltpu.VMEM((PREP_TM, Z_WIDTH), F32), pltpu.VMEM((PREP_TM, D_MODEL), BF16)],
        compiler_params=_params("parallel"),
        name="mixer_prep",
    )(x, g.reshape(1, D_MODEL), w_in, tabs, qn.reshape(1, -1), kvn.reshape(1, -1), wuq, wukv)


ATT_TQ = MOBA_BLOCK
ATT_TK = 2 * MOBA_BLOCK
ATT_STEP_TILES = 4


def _split_bf16(x):
    hi = x.astype(BF16)
    lo = (x - hi.astype(F32)).astype(BF16)
    return hi, lo


MASKED = -1e30


def _moba_gate_logits(q, kbar, qi):
    nblk = SEQ // MOBA_BLOCK
    nt = (((1,), (1,)), ((), ()))
    kb_hi, kb_lo = _split_bf16(kbar)
    gate = (lax.dot_general(kb_hi, q, nt, preferred_element_type=F32)
            + lax.dot_general(kb_lo, q, nt, preferred_element_type=F32))[:nblk]
    blk = lax.broadcasted_iota(jnp.int32, gate.shape, 0)
    gate = jnp.where(blk < qi, gate, NEG_INF)
    rank = jnp.zeros(gate.shape, jnp.int32)
    for other in range(nblk):
        row = gate[other:other + 1, :]
        beats = (row > gate) | ((row == gate) & (other < blk))
        rank = rank + jnp.where(beats, 1, 0)
    keep = ((rank < MOBA_TOPK) & (blk < qi)) | (blk == qi)
    logit_t = jnp.where(keep, 0.0, MASKED)
    logit_t = jnp.concatenate([logit_t, jnp.full((LANES - nblk, gate.shape[1]), MASKED, F32)], axis=0)
    return logit_t.T


def _attn_kernel(q_ref, k_ref, v_ref, o_ref, m_ref, mnext_ref, acc_ref, s_ref, *moba_scratch,
                 n_heads, dq, dv, moba):
    step = pl.program_id(1)
    if moba:
        kbar_ref, qaug_ref = moba_scratch

        @pl.when(step == 0)
        def _():
            kbar_ref[...] = jnp.zeros_like(kbar_ref)

            def body(b, carry):
                sl = pl.ds(pl.multiple_of(b * MOBA_BLOCK, MOBA_BLOCK), MOBA_BLOCK)
                kbar_ref[pl.ds(b, 1), :] = jnp.mean(k_ref[sl, :].astype(F32), axis=0, keepdims=True)
                return carry

            lax.fori_loop(0, SEQ // MOBA_BLOCK, body, 0)

    for sub in range(ATT_STEP_TILES):
        rows = slice(sub * ATT_TQ, (sub + 1) * ATT_TQ)
        _attn_tile(step * ATT_STEP_TILES + sub, q_ref.at[rows], k_ref, v_ref, o_ref.at[rows],
                   m_ref, mnext_ref, acc_ref, s_ref, *moba_scratch, n_heads=n_heads, dq=dq, dv=dv, moba=moba)


def _attn_tile(qi, q_ref, k_ref, v_ref, o_ref, m_ref, mnext_ref, acc_ref, s_ref, *moba_scratch,
               n_heads, dq, dv, moba):
    nt = (((1,), (1,)), ((), ()))
    if moba:
        kbar_ref, qaug_ref = moba_scratch
        for hd in range(n_heads):
            qh = q_ref[:, hd * HEAD_DIM:(hd + 1) * HEAD_DIM]
            logits = _moba_gate_logits(qh, kbar_ref[:, hd * dq:hd * dq + HEAD_DIM], qi)
            qaug_ref[:, hd * dq:hd * dq + HEAD_DIM] = qh
            qaug_ref[:, hd * dq + HEAD_DIM:(hd + 1) * dq] = logits.astype(BF16)
        q_src = qaug_ref
    else:
        q_src = q_ref

    def tile(t):
        return pl.ds(pl.multiple_of(t * ATT_TK, ATT_TK), ATT_TK)

    r = lax.broadcasted_iota(jnp.int32, (ATT_TQ, ATT_TQ), 0)
    c = lax.broadcasted_iota(jnp.int32, (ATT_TQ, ATT_TQ), 1)
    tri = c <= r
    if ATT_TK == ATT_TQ:
        t_diag = qi
        mask = tri
    else:
        t_diag = qi // 2
        odd = (qi % 2) == 1
        mask = jnp.concatenate([tri | odd, tri & odd], axis=1)
    dva = 2 * dv

    def scores(t, hd):
        return lax.dot_general(q_src[:, hd * dq:(hd + 1) * dq], k_ref[tile(t), hd * dq:(hd + 1) * dq], nt,
                               preferred_element_type=F32)

    def weighted_values(p, t, hd):
        return jnp.dot(p.astype(BF16), v_ref[tile(t), hd * dva:(hd + 1) * dva], preferred_element_type=F32)

    heads = range(n_heads)
    for hd in heads:
        s = jnp.where(mask, scores(t_diag, hd), NEG_INF)
        m = jnp.max(s, axis=-1, keepdims=True)
        m_ref[hd] = m
        acc_ref[hd] = weighted_values(jnp.exp(s - m), t_diag, hd)

    def body(t, carry):
        t_next = jnp.minimum(t + 1, t_diag - 1)
        for hd in heads:
            m_cur = mnext_ref[hd]
            p = jnp.exp(s_ref[hd] - m_cur)
            s = scores(t_next, hd)
            s_ref[hd] = s
            mnext_ref[hd] = jnp.maximum(m_cur, jnp.max(s, axis=-1, keepdims=True))
            alpha = jnp.exp(m_ref[hd] - m_cur)
            acc_ref[hd] = alpha * acc_ref[hd] + weighted_values(p, t, hd)
            m_ref[hd] = m_cur
        return carry

    for hd in heads:
        s = scores(0, hd)
        s_ref[hd] = s
        mnext_ref[hd] = jnp.maximum(m_ref[hd], jnp.max(s, axis=-1, keepdims=True))
    lax.fori_loop(0, t_diag, body, 0)
    for hd in heads:
        acc = acc_ref[hd]
        o_ref[:, hd * dv:(hd + 1) * dv] = (acc[:, :dv] / acc[:, dv:]).astype(o_ref.dtype)


def _attention(packed, q_off, q_width, k_off, v_off, n_heads, dq, dv, moba):
    step_rows = ATT_STEP_TILES * ATT_TQ
    nq = SEQ // step_rows
    k_width, v_width = n_heads * dq, n_heads * 2 * dv
    q_blk, k_blk, v_blk = q_off // q_width, k_off // k_width, v_off // v_width
    scratch = [pltpu.VMEM((n_heads, ATT_TQ, 1), F32), pltpu.VMEM((n_heads, ATT_TQ, 1), F32),
               pltpu.VMEM((n_heads, ATT_TQ, 2 * dv), F32), pltpu.VMEM((n_heads, ATT_TQ, ATT_TK), F32)]
    if moba:
        scratch += [pltpu.VMEM((LANES, n_heads * dq), F32), pltpu.VMEM((ATT_TQ, n_heads * dq), BF16)]
    return pl.pallas_call(
        functools.partial(_attn_kernel, n_heads=n_heads, dq=dq, dv=dv, moba=moba),
        grid=(BATCH, nq),
        in_specs=[
            pl.BlockSpec((step_rows, q_width), lambda b, i: (b * nq + i, q_blk)),
            pl.BlockSpec((SEQ, k_width), lambda b, i: (b, k_blk)),
            pl.BlockSpec((SEQ, v_width), lambda b, i: (b, v_blk)),
        ],
        out_specs=pl.BlockSpec((step_rows, n_heads * dv), lambda b, i: (b * nq + i, 0)),
        out_shape=jax.ShapeDtypeStruct((TOKENS, n_heads * dv), BF16),
        scratch_shapes=scratch,
        compiler_params=_params("parallel", "arbitrary"),
        name="moba_attn" if moba else "mla_attn",
    )(packed, packed, packed)


SWA_STEP_BLOCKS = 4


OUT_TN = 512
W_MLA = N_MLA_HEADS * MLA_V_DIM
W_MOBA = N_MOBA_HEADS * HEAD_DIM
W_SWA = N_SWA_HEADS * HEAD_DIM
MIX_WIDTH = W_MLA + W_MOBA + W_SWA
SWA_STEP_ROWS = SWA_STEP_BLOCKS * SWA_WINDOW


def _swa_out_kernel(sink_ref, x_ref, a_ref, b_ref, q_ref, kp_ref, kc_ref, vp_ref, vc_ref, w_ref, o_ref,
                    w16_ref, s_ref, p_ref, m_ref, c_ref):
    step = pl.program_id(0)
    n = step % (SEQ // SWA_STEP_ROWS)
    W = SWA_WINDOW
    G = SWA_GROUP
    nt = (((1,), (1,)), ((), ()))
    blocks = range(SWA_STEP_BLOCKS)
    col_chunks = [slice(c * OUT_TN, (c + 1) * OUT_TN) for c in range(D_MODEL // OUT_TN)]

    @pl.when(step == 0)
    def _():
        def body(r, carry):
            sl = pl.ds(pl.multiple_of(r * NORM_ROWS, NORM_ROWS), NORM_ROWS)
            w16_ref[sl, :] = w_ref[sl, :].astype(BF16)
            return carry

        lax.fori_loop(0, MIX_WIDTH // NORM_ROWS, body, 0)

    r = lax.broadcasted_iota(jnp.int32, (G * W, 2 * W), 0) % W
    c = lax.broadcasted_iota(jnp.int32, (G * W, 2 * W), 1)
    rel = r + W - c
    band = (rel >= 0) & (rel < W)

    def project(lhs, w_rows, cols, first):
        part = jnp.dot(lhs, w16_ref[w_rows, cols], preferred_element_type=F32)
        if first:
            o_ref[:, cols] = x_ref[:, cols] + part
        else:
            o_ref[:, cols] += part

    fillers = [
        [functools.partial(project, a_ref[...], slice(0, W_MLA), cols, True) for cols in col_chunks],
        [functools.partial(project, b_ref[...], slice(W_MLA, W_MLA + W_MOBA), cols, False) for cols in col_chunks],
    ]

    for hk in range(N_SWA_KV_HEADS):
        kcols = slice(hk * HEAD_DIM, (hk + 1) * HEAD_DIM)
        vcols = slice(hk * 2 * HEAD_DIM, (hk + 1) * 2 * HEAD_DIM)
        kk = jnp.concatenate([kp_ref[:, kcols], kc_ref[:, kcols]], axis=0)
        vv = jnp.concatenate([vp_ref[:, vcols], vc_ref[:, vcols]], axis=0)
        sink = jnp.concatenate([jnp.full((W, 1), sink_ref[hk * G + g], F32) for g in range(G)], axis=0)
        fill = fillers[hk]
        for blk in blocks:
            q = jnp.concatenate([q_ref[blk * W:(blk + 1) * W, (hk * G + g) * LANES:(hk * G + g + 1) * LANES]
                                 for g in range(G)], axis=0)
            s = lax.dot_general(q, kk[blk * W:(blk + 2) * W], nt, preferred_element_type=F32)
            mask = band & ((c >= W) | (n > 0)) if blk == 0 else band
            s_ref[blk] = jnp.where(mask, s, NEG_INF)
        fill[0]()
        for blk in blocks:
            m_ref[blk] = jnp.maximum(jnp.max(s_ref[blk], axis=-1, keepdims=True), sink)
        fill[1]()
        for blk in blocks:
            p_ref[blk] = jnp.exp(s_ref[blk] - m_ref[blk]).astype(BF16)
        fill[2]()
        for blk in blocks:
            pv = jnp.dot(p_ref[blk], vv[blk * W:(blk + 2) * W], preferred_element_type=F32)
            o = pv[:, :LANES] / (pv[:, LANES:] + jnp.exp(sink - m_ref[blk]))
            for g in range(G):
                c_ref[blk * W:(blk + 1) * W, (hk * G + g) * LANES:(hk * G + g + 1) * LANES] = (
                    o[g * W:(g + 1) * W].astype(BF16))
        fill[3]()

    for cols in col_chunks:
        project(c_ref[...], slice(W_MLA + W_MOBA, MIX_WIDTH), cols, False)


def _swa_out(sinks, x, a, b, packed, w, layer):
    W = SWA_WINDOW
    nsteps = SEQ // SWA_STEP_ROWS
    kw = N_SWA_KV_HEADS * HEAD_DIM
    q_blk, k_blk, v_blk = P_SWA_Q // W_SWA, P_SWA_K // kw, P_SWA_V // (2 * kw)
    cur = lambda i: (i, 0)

    def prev_row(i):
        return (i // nsteps) * (SEQ // W) + jnp.maximum((i % nsteps) * SWA_STEP_BLOCKS - 1, 0)

    return pl.pallas_call(
        _swa_out_kernel,
        grid=(TOKENS // SWA_STEP_ROWS,),
        in_specs=[
            pl.BlockSpec(memory_space=pltpu.SMEM),
            pl.BlockSpec((SWA_STEP_ROWS, D_MODEL), cur),
            pl.BlockSpec((SWA_STEP_ROWS, W_MLA), cur),
            pl.BlockSpec((SWA_STEP_ROWS, W_MOBA), cur),
            pl.BlockSpec((SWA_STEP_ROWS, W_SWA), lambda i: (i, q_blk)),
            pl.BlockSpec((W, kw), lambda i: (prev_row(i), k_blk)),
            pl.BlockSpec((SWA_STEP_ROWS, kw), lambda i: (i, k_blk)),
            pl.BlockSpec((W, 2 * kw), lambda i: (prev_row(i), v_blk)),
            pl.BlockSpec((SWA_STEP_ROWS, 2 * kw), lambda i: (i, v_blk)),
            pl.BlockSpec((None, MIX_WIDTH, D_MODEL), lambda i: (layer, 0, 0), pipeline_mode=pl.Buffered(1)),
        ],
        out_specs=pl.BlockSpec((SWA_STEP_ROWS, D_MODEL), cur),
        out_shape=jax.ShapeDtypeStruct((TOKENS, D_MODEL), F32),
        scratch_shapes=[pltpu.VMEM((MIX_WIDTH, D_MODEL), BF16),
                        pltpu.VMEM((SWA_STEP_BLOCKS, SWA_GROUP * W, 2 * W), F32),
                        pltpu.VMEM((SWA_STEP_BLOCKS, SWA_GROUP * W, 2 * W), BF16),
                        pltpu.VMEM((SWA_STEP_BLOCKS, SWA_GROUP * W, 1), F32),
                        pltpu.VMEM((SWA_STEP_ROWS, W_SWA), BF16)],
        compiler_params=_params("arbitrary"),
        name="swa_out",
    )(sinks, x, a, b, packed, packed, packed, packed, packed, w)


def _rope_tables():
    def angles(d):
        half = d // 2
        inv_freq = 1.0 / (ROPE_THETA ** (jnp.arange(half, dtype=F32) * (2.0 / d)))
        return jnp.arange(SEQ, dtype=F32)[:, None] * inv_freq[None, :]

    ang = angles(HEAD_DIM)
    cos = jnp.concatenate([jnp.cos(ang), jnp.cos(ang)], axis=1)
    sin = jnp.concatenate([-jnp.sin(ang), jnp.sin(ang)], axis=1)
    angm = angles(MLA_ROPE_DIM)
    zero = jnp.zeros_like(angm)
    cosm = jnp.concatenate([jnp.cos(angm), zero, jnp.cos(angm), zero], axis=1)
    sinm = jnp.concatenate([-jnp.sin(angm), zero, jnp.sin(angm), zero], axis=1)
    return jnp.concatenate([cos, sin, cosm, sinm], axis=1)


def _spread_rope_cols(w):
    half = MLA_ROPE_DIM // 2
    zero = jnp.zeros(w.shape[:-1] + (half,), w.dtype)
    return jnp.concatenate([w[..., :half], zero, w[..., half:], zero], axis=-1)


LAYOUT_COLS = 512


def _layout_w_in_kernel(w_ref, o_ref):
    half = MLA_ROPE_DIM // 2
    src = Z_KROPE + 2 * half
    o_ref[0:Z_KROPE + half, :] = w_ref[0:Z_KROPE + half, :].astype(BF16)
    o_ref[Z_KROPE + half:Z_KROPE + 2 * half, :] = jnp.zeros((half, LAYOUT_COLS), BF16)
    o_ref[Z_KROPE + 2 * half:Z_KROPE + 3 * half, :] = w_ref[Z_KROPE + half:src, :].astype(BF16)
    o_ref[Z_KROPE + 3 * half:Z_MQ, :] = jnp.zeros((half, LAYOUT_COLS), BF16)
    n_rest = w_ref.shape[0] - src
    o_ref[Z_MQ:Z_MQ + n_rest, :] = w_ref[src:, :].astype(BF16)
    o_ref[Z_MQ + n_rest:, :] = jnp.zeros((Z_WIDTH - Z_MQ - n_rest, LAYOUT_COLS), BF16)


def _layout_w_in(w):
    wt = jnp.swapaxes(w, 1, 2)
    return pl.pallas_call(
        _layout_w_in_kernel,
        grid=(DEPTH, D_MODEL // LAYOUT_COLS),
        in_specs=[pl.BlockSpec((None, wt.shape[1], LAYOUT_COLS), lambda l, i: (l, 0, i))],
        out_specs=pl.BlockSpec((None, Z_WIDTH, LAYOUT_COLS), lambda l, i: (l, 0, i)),
        out_shape=jax.ShapeDtypeStruct((DEPTH, Z_WIDTH, D_MODEL), BF16),
        compiler_params=_params("parallel", "parallel"),
        name="layout_w_in",
    )(wt)


def _layout_w_uq(w):
    w = w.astype(BF16).reshape(DEPTH, MLA_Q_LORA, N_MLA_HEADS, MLA_NOPE_DIM + MLA_ROPE_DIM)
    w = jnp.concatenate([w[..., :MLA_NOPE_DIM], _spread_rope_cols(w[..., MLA_NOPE_DIM:])], axis=-1)
    return w.reshape(DEPTH, MLA_Q_LORA, N_MLA_HEADS * MLA_QK_PAD)


def kernel(x, ffn1_norm, ffn1_w_gate, ffn1_w_up, ffn1_w_down, attn_norm, w_in, mla_q_norm, mla_w_uq, mla_kv_norm, mla_w_ukv, swa_sinks, w_out, ffn2_norm, ffn2_w_gate, ffn2_w_up, ffn2_w_down, final_norm):
    tabs = _rope_tables()
    x = x.reshape(TOKENS, D_MODEL)
    ffn1_g = ffn1_norm.reshape(DEPTH, 1, D_MODEL)
    ffn2_g = ffn2_norm.reshape(DEPTH, 1, D_MODEL)
    w_in16 = _layout_w_in(w_in)
    w_uq16 = _layout_w_uq(mla_w_uq)
    w_ukv16 = mla_w_ukv.astype(BF16)
    final_g = final_norm.reshape(1, D_MODEL)
    for l in range(DEPTH):
        x = _ffn(x, ffn1_g, ffn1_w_gate, ffn1_w_up, ffn1_w_down, l, final_g)
        packed = _prep(x, attn_norm[l], w_in16, tabs, mla_q_norm[l], mla_kv_norm[l], w_uq16, w_ukv16, l)
        o_mla = _attention(packed, P_MLA_Q, N_MLA_HEADS * MLA_QK_PAD, P_MLA_K, P_MLA_V,
                           N_MLA_HEADS, MLA_QK_PAD, MLA_V_DIM, moba=False)
        o_moba = _attention(packed, P_MOBA_Q, N_MOBA_HEADS * HEAD_DIM, P_MOBA_K, P_MOBA_V,
                            N_MOBA_HEADS, MOBA_QK_AUG, HEAD_DIM, moba=True)
        x = _swa_out(swa_sinks[l], x, o_mla, o_moba, packed, w_out, l)
        x = _ffn(x, ffn2_g, ffn2_w_gate, ffn2_w_up, ffn2_w_down, l, final_g, post_norm=(l == DEPTH - 1))
    return x.reshape(BATCH, SEQ, D_MODEL)
```

```python
import functools

import jax
import jax.numpy as jnp
import numpy as np
from jax import lax
from jax.experimental import pallas as pl
from jax.experimental.pallas import tpu as pltpu

D_MODEL = 2048
BATCH = 2
SEQ = 4096
DEPTH = 2
TOKENS = BATCH * SEQ

HEAD_DIM = 128
N_MLA_HEADS = 4
MLA_Q_LORA = 512
MLA_KV_LORA = 256
MLA_NOPE_DIM = 128
MLA_ROPE_DIM = 64
MLA_V_DIM = 128
MLA_QK_PAD = 256
N_MOBA_HEADS = 4
MOBA_BLOCK = 256
MOBA_TOPK = 3
MOBA_QK_AUG = 256
N_SWA_HEADS = 8
N_SWA_KV_HEADS = 2
SWA_GROUP = N_SWA_HEADS // N_SWA_KV_HEADS
SWA_WINDOW = 128
D_FF = 5632
ROPE_THETA = 10000.0
NORM_EPS = 1e-6

LANES = 128
Z_WIDTH = 4096

Z_CQ = 0
Z_CKV = 512
Z_KROPE = 768
Z_MQ = 896
Z_MK = 1408
Z_MV = 1920
Z_SQ = 2432
Z_SK = 3456
Z_SV = 3712

VMEM_LIMIT = 56 * 1024 * 1024

BF16 = jnp.bfloat16
F32 = jnp.float32
NEG_INF = float("-inf")


def _params(*sem, flags=None):
    return pltpu.CompilerParams(dimension_semantics=sem, vmem_limit_bytes=VMEM_LIMIT, flags=flags)


FFN_TM = 1024
FFN_TF = 256
FFN_TAIL_CHUNKS = 2
FFN_DOWN_TN = 512
NORM_ROWS = 128


def _rmsnorm_rows(x, g):
    ms = jnp.mean(x * x, axis=-1, keepdims=True)
    return x * lax.rsqrt(ms + NORM_EPS) * g


def _norm_into(x_ref, g_ref, h_ref, rows, copy_ref=None):
    g = g_ref[...]

    def body(r, carry):
        sl = pl.ds(pl.multiple_of(r * NORM_ROWS, NORM_ROWS), NORM_ROWS)
        x = x_ref[sl, :]
        h_ref[sl, :] = _rmsnorm_rows(x, g).astype(BF16)
        if copy_ref is not None:
            copy_ref[sl, :] = x
        return carry

    lax.fori_loop(0, rows // NORM_ROWS, body, 0)


def _ffn_chunk(h, wg, wu, wd_cols, o_ref):
    gate = jnp.dot(h, wg, preferred_element_type=F32)
    up = jnp.dot(h, wu, preferred_element_type=F32)
    act = (gate * jax.nn.sigmoid(gate) * (0.5 * up)).astype(BF16)
    for c in range(D_MODEL // FFN_DOWN_TN):
        cols = slice(c * FFN_DOWN_TN, (c + 1) * FFN_DOWN_TN)
        o_ref[:, cols] += jnp.dot(act, wd_cols(cols), preferred_element_type=F32)


def _norm_in_place(o_ref, g_ref, rows):
    g = g_ref[...]

    def body(r, carry):
        sl = pl.ds(pl.multiple_of(r * NORM_ROWS, NORM_ROWS), NORM_ROWS)
        o_ref[sl, :] = _rmsnorm_rows(o_ref[sl, :], g)
        return carry

    lax.fori_loop(0, rows // NORM_ROWS, body, 0)


def _ffn_head_kernel(x_ref, g_ref, post_g_ref, wg_ref, wu_ref, wd_ref, o_ref, wg16_ref, wu16_ref, wd16_ref, h_ref,
                     *, post_norm):
    @pl.when(pl.program_id(0) == 0)
    def _():
        _norm_into(x_ref, g_ref, h_ref, FFN_TM, copy_ref=o_ref)

    wg16_ref[...] = wg_ref[...].astype(BF16)
    wu16_ref[...] = wu_ref[...].astype(BF16)
    wd16_ref[...] = wd_ref[...].astype(BF16)
    _ffn_chunk(h_ref[...], wg16_ref[...], wu16_ref[...], lambda cols: wd16_ref[:, cols], o_ref)

    if post_norm:
        @pl.when(pl.program_id(0) == pl.num_programs(0) - 1)
        def _():
            _norm_in_place(o_ref, post_g_ref, FFN_TM)


def _ffn_tail_kernel(x_ref, g_ref, post_g_ref, wg16_ref, wu16_ref, wd16_ref, o_ref, h_ref, *, post_norm):
    @pl.when(pl.program_id(1) == 0)
    def _():
        _norm_into(x_ref, g_ref, h_ref, FFN_TM, copy_ref=o_ref)

    for c in range(FFN_TAIL_CHUNKS):
        rows = slice(c * FFN_TF, (c + 1) * FFN_TF)
        _ffn_chunk(h_ref[...], wg16_ref[c], wu16_ref[c],
                   lambda cols, rows=rows: wd16_ref[rows, cols], o_ref)

    if post_norm:
        @pl.when(pl.program_id(1) == pl.num_programs(1) - 1)
        def _():
            _norm_in_place(o_ref, post_g_ref, FFN_TM)


def _ffn(x, g, wg, wu, wd, layer, post_g, post_norm=False):
    n_chunks = D_FF // FFN_TF
    out, wg16, wu16, wd16 = pl.pallas_call(
        functools.partial(_ffn_head_kernel, post_norm=post_norm),
        grid=(n_chunks,),
        in_specs=[
            pl.BlockSpec((FFN_TM, D_MODEL), lambda j: (0, 0)),
            pl.BlockSpec((None, 1, D_MODEL), lambda j: (layer, 0, 0)),
            pl.BlockSpec((1, D_MODEL), lambda j: (0, 0)),
            pl.BlockSpec((None, D_MODEL, FFN_TF), lambda j: (layer, 0, j)),
            pl.BlockSpec((None, D_MODEL, FFN_TF), lambda j: (layer, 0, j)),
            pl.BlockSpec((None, FFN_TF, D_MODEL), lambda j: (layer, j, 0)),
        ],
        out_specs=[
            pl.BlockSpec((FFN_TM, D_MODEL), lambda j: (0, 0)),
            pl.BlockSpec((None, D_MODEL, FFN_TF), lambda j: (j, 0, 0)),
            pl.BlockSpec((None, D_MODEL, FFN_TF), lambda j: (j, 0, 0)),
            pl.BlockSpec((FFN_TF, D_MODEL), lambda j: (j, 0)),
        ],
        out_shape=[
            jax.ShapeDtypeStruct((TOKENS, D_MODEL), F32),
            jax.ShapeDtypeStruct((n_chunks, D_MODEL, FFN_TF), BF16),
            jax.ShapeDtypeStruct((n_chunks, D_MODEL, FFN_TF), BF16),
            jax.ShapeDtypeStruct((D_FF, D_MODEL), BF16),
        ],
        input_output_aliases={0: 0},
        scratch_shapes=[pltpu.VMEM((FFN_TM, D_MODEL), BF16)],
        compiler_params=_params("arbitrary"),
        name="ffn_head",
    )(x, g, post_g, wg, wu, wd)
    return pl.pallas_call(
        functools.partial(_ffn_tail_kernel, post_norm=post_norm),
        grid=(TOKENS // FFN_TM - 1, n_chunks // FFN_TAIL_CHUNKS),
        in_specs=[
            pl.BlockSpec((FFN_TM, D_MODEL), lambda i, j: (i + 1, 0)),
            pl.BlockSpec((None, 1, D_MODEL), lambda i, j: (layer, 0, 0)),
            pl.BlockSpec((1, D_MODEL), lambda i, j: (0, 0)),
            pl.BlockSpec((FFN_TAIL_CHUNKS, D_MODEL, FFN_TF), lambda i, j: (j, 0, 0)),
            pl.BlockSpec((FFN_TAIL_CHUNKS, D_MODEL, FFN_TF), lambda i, j: (j, 0, 0)),
            pl.BlockSpec((FFN_TAIL_CHUNKS * FFN_TF, D_MODEL), lambda i, j: (j, 0)),
        ],
        out_specs=pl.BlockSpec((FFN_TM, D_MODEL), lambda i, j: (i + 1, 0)),
        out_shape=jax.ShapeDtypeStruct((TOKENS, D_MODEL), F32),
        input_output_aliases={0: 0},
        scratch_shapes=[pltpu.VMEM((FFN_TM, D_MODEL), BF16)],
        compiler_params=_params("parallel", "arbitrary"),
        name="ffn_tail",
    )(out, g, post_g, wg16, wu16, wd16)


PROJ_TN = 512
PREP_TM = 512


def _project_into(x_ref, g_ref, w_ref, z_ref, h_ref):
    _norm_into(x_ref, g_ref, h_ref, PREP_TM)
    nt = (((1,), (1,)), ((), ()))
    for c in range(Z_WIDTH // PROJ_TN):
        cols = slice(c * PROJ_TN, (c + 1) * PROJ_TN)
        z_ref[:, cols] = lax.dot_general(h_ref[...], w_ref[cols, :], nt, preferred_element_type=F32)


def _rope(x, cos, sin_signed):
    return x * cos + pltpu.roll(x, LANES // 2, axis=1) * sin_signed


P_MLA_Q = 0
P_MLA_K = P_MLA_Q + N_MLA_HEADS * MLA_QK_PAD
P_MLA_V = P_MLA_K + N_MLA_HEADS * MLA_QK_PAD
P_MOBA_K = P_MLA_V + N_MLA_HEADS * 2 * MLA_V_DIM
P_MOBA_V = P_MOBA_K + N_MOBA_HEADS * MOBA_QK_AUG
P_SWA_Q = P_MOBA_V + N_MOBA_HEADS * 2 * HEAD_DIM
P_MOBA_Q = P_SWA_Q + N_SWA_HEADS * HEAD_DIM
P_SWA_V = P_MOBA_Q + N_MOBA_HEADS * HEAD_DIM
P_SWA_K = P_SWA_V + N_SWA_KV_HEADS * 2 * HEAD_DIM
P_WIDTH = P_SWA_K + N_SWA_KV_HEADS * HEAD_DIM


def _prep_kernel(x_ref, g_ref, w_ref, tab_ref, qn_ref, kvn_ref, wuq_ref, wukv_ref, p_ref, z_ref, h_ref):
    def section(offset, width):
        return p_ref.at[:, offset:offset + width]

    mlaq_ref = section(P_MLA_Q, N_MLA_HEADS * MLA_QK_PAD)
    mlak_ref = section(P_MLA_K, N_MLA_HEADS * MLA_QK_PAD)
    mlav_ref = section(P_MLA_V, N_MLA_HEADS * 2 * MLA_V_DIM)
    mq_ref = section(P_MOBA_Q, N_MOBA_HEADS * HEAD_DIM)
    mk_ref = section(P_MOBA_K, N_MOBA_HEADS * MOBA_QK_AUG)
    mv_ref = section(P_MOBA_V, N_MOBA_HEADS * 2 * HEAD_DIM)
    sq_ref = section(P_SWA_Q, N_SWA_HEADS * HEAD_DIM)
    sk_ref = section(P_SWA_K, N_SWA_KV_HEADS * HEAD_DIM)
    sv_ref = section(P_SWA_V, N_SWA_KV_HEADS * 2 * HEAD_DIM)
    _project_into(x_ref, g_ref, w_ref, z_ref, h_ref)
    cos, sin, cosm, sinm = (tab_ref[:, t * LANES:(t + 1) * LANES] for t in range(4))
    mla_scale = (MLA_NOPE_DIM + MLA_ROPE_DIM) ** -0.5
    scale = HEAD_DIM ** -0.5

    cq = _rmsnorm_rows(z_ref[:, Z_CQ:Z_CQ + MLA_Q_LORA], qn_ref[...]).astype(BF16)
    q = jnp.dot(cq, wuq_ref[...], preferred_element_type=F32)
    for hd in range(N_MLA_HEADS):
        base = hd * MLA_QK_PAD
        mlaq_ref[:, base:base + LANES] = (q[:, base:base + LANES] * mla_scale).astype(BF16)
        pe = _rope(q[:, base + LANES:base + 2 * LANES], cosm, sinm)
        mlaq_ref[:, base + LANES:base + 2 * LANES] = (pe * mla_scale).astype(BF16)

    ones = jnp.ones((PREP_TM, LANES), BF16)
    ckv = _rmsnorm_rows(z_ref[:, Z_CKV:Z_CKV + MLA_KV_LORA], kvn_ref[...]).astype(BF16)
    kv = jnp.dot(ckv, wukv_ref[...], preferred_element_type=F32)
    kpe = _rope(z_ref[:, Z_KROPE:Z_KROPE + LANES], cosm, sinm).astype(BF16)
    for hd in range(N_MLA_HEADS):
        base = hd * MLA_QK_PAD
        mlak_ref[:, base:base + LANES] = kv[:, base:base + LANES].astype(BF16)
        mlak_ref[:, base + LANES:base + 2 * LANES] = kpe
        mlav_ref[:, base:base + LANES] = kv[:, base + LANES:base + 2 * LANES].astype(BF16)
        mlav_ref[:, base + LANES:base + 2 * LANES] = ones

    pos = (pl.program_id(0) % (SEQ // PREP_TM)) * PREP_TM + lax.broadcasted_iota(jnp.int32, (PREP_TM, LANES), 0)
    lane = lax.broadcasted_iota(jnp.int32, (PREP_TM, LANES), 1)
    block_onehot = jnp.where(pos // MOBA_BLOCK == lane, 1.0, 0.0).astype(BF16)
    for hd in range(N_MOBA_HEADS):
        c = hd * LANES
        mq_ref[:, c:c + LANES] = (_rope(z_ref[:, Z_MQ + c:Z_MQ + c + LANES], cos, sin) * scale).astype(BF16)
        mk_ref[:, 2 * c:2 * c + LANES] = _rope(z_ref[:, Z_MK + c:Z_MK + c + LANES], cos, sin).astype(BF16)
        mk_ref[:, 2 * c + LANES:2 * c + 2 * LANES] = block_onehot
        mv_ref[:, 2 * c:2 * c + LANES] = z_ref[:, Z_MV + c:Z_MV + c + LANES].astype(BF16)
        mv_ref[:, 2 * c + LANES:2 * c + 2 * LANES] = ones

    for hd in range(N_SWA_HEADS):
        c = hd * LANES
        sq_ref[:, c:c + LANES] = (_rope(z_ref[:, Z_SQ + c:Z_SQ + c + LANES], cos, sin) * scale).astype(BF16)
    for hd in range(N_SWA_KV_HEADS):
        c = hd * LANES
        sk_ref[:, c:c + LANES] = _rope(z_ref[:, Z_SK + c:Z_SK + c + LANES], cos, sin).astype(BF16)
        sv_ref[:, 2 * c:2 * c + LANES] = z_ref[:, Z_SV + c:Z_SV + c + LANES].astype(BF16)
        sv_ref[:, 2 * c + LANES:2 * c + 2 * LANES] = ones


def _prep(x, g, w_in, tabs, qn, kvn, wuq, wukv, layer):
    nblk = SEQ // PREP_TM
    row = lambda i: (i, 0)
    tab = lambda i: (i % nblk, 0)
    const = lambda i: (0, 0)
    of_layer = lambda i: (layer, 0, 0)
    return pl.pallas_call(
        _prep_kernel,
        grid=(TOKENS // PREP_TM,),
        in_specs=[
            pl.BlockSpec((PREP_TM, D_MODEL), row),
            pl.BlockSpec((1, D_MODEL), const),
            pl.BlockSpec((None, Z_WIDTH, D_MODEL), of_layer, pipeline_mode=pl.Buffered(1)),
            pl.BlockSpec((PREP_TM, 4 * LANES), tab),
            pl.BlockSpec((1, MLA_Q_LORA), const),
            pl.BlockSpec((1, MLA_KV_LORA), const),
            pl.BlockSpec((None, MLA_Q_LORA, N_MLA_HEADS * MLA_QK_PAD), of_layer),
            pl.BlockSpec((None, MLA_KV_LORA, N_MLA_HEADS * MLA_QK_PAD), of_layer),
        ],
        out_specs=pl.BlockSpec((PREP_TM, P_WIDTH), row),
        out_shape=jax.ShapeDtypeStruct((TOKENS, P_WIDTH), BF16),
        scratch_shapes=[pltpu.VMEM((PREP_TM, Z_WIDTH), F32), pltpu.VMEM((PREP_TM, D_MODEL), BF16)],
        compiler_params=_params("parallel"),
        name="mixer_prep",
    )(x, g.reshape(1, D_MODEL), w_in, tabs, qn.reshape(1, -1), kvn.reshape(1, -1), wuq, wukv)


ATT_TQ = MOBA_BLOCK
ATT_TK = 2 * MOBA_BLOCK
ATT_STEP_TILES = 4


def _split_bf16(x):
    hi = x.astype(BF16)
    lo = (x - hi.astype(F32)).astype(BF16)
    return hi, lo


MASKED = -1e30


def _moba_gate_logits(q, kbar, qi):
    nblk = SEQ // MOBA_BLOCK
    nt = (((1,), (1,)), ((), ()))
    kb_hi, kb_lo = _split_bf16(kbar)
    gate = (lax.dot_general(kb_hi, q, nt, preferred_element_type=F32)
            + lax.dot_general(kb_lo, q, nt, preferred_element_type=F32))[:nblk]
    blk = lax.broadcasted_iota(jnp.int32, gate.shape, 0)
    gate = jnp.where(blk < qi, gate, NEG_INF)
    rank = jnp.zeros(gate.shape, jnp.int32)
    for other in range(nblk):
        row = gate[other:other + 1, :]
        beats = (row > gate) | ((row == gate) & (other < blk))
        rank = rank + jnp.where(beats, 1, 0)
    keep = ((rank < MOBA_TOPK) & (blk < qi)) | (blk == qi)
    logit_t = jnp.where(keep, 0.0, MASKED)
    logit_t = jnp.concatenate([logit_t, jnp.full((LANES - nblk, gate.shape[1]), MASKED, F32)], axis=0)
    return logit_t.T


def _attn_kernel(q_ref, k_ref, v_ref, o_ref, m_ref, mnext_ref, acc_ref, s_ref, *moba_scratch,
                 n_heads, dq, dv, moba):
    step = pl.program_id(1)
    if moba:
        kbar_ref, qaug_ref = moba_scratch

        @pl.when(step == 0)
        def _():
            kbar_ref[...] = jnp.zeros_like(kbar_ref)

            def body(b, carry):
                sl = pl.ds(pl.multiple_of(b * MOBA_BLOCK, MOBA_BLOCK), MOBA_BLOCK)
                kbar_ref[pl.ds(b, 1), :] = jnp.mean(k_ref[sl, :].astype(F32), axis=0, keepdims=True)
                return carry

            lax.fori_loop(0, SEQ // MOBA_BLOCK, body, 0)

    for sub in range(ATT_STEP_TILES):
        rows = slice(sub * ATT_TQ, (sub + 1) * ATT_TQ)
        _attn_tile(step * ATT_STEP_TILES + sub, q_ref.at[rows], k_ref, v_ref, o_ref.at[rows],
                   m_ref, mnext_ref, acc_ref, s_ref, *moba_scratch, n_heads=n_heads, dq=dq, dv=dv, moba=moba)


def _attn_tile(qi, q_ref, k_ref, v_ref, o_ref, m_ref, mnext_ref, acc_ref, s_ref, *moba_scratch,
               n_heads, dq, dv, moba):
    nt = (((1,), (1,)), ((), ()))
    if moba:
        kbar_ref, qaug_ref = moba_scratch
        for hd in range(n_heads):
            qh = q_ref[:, hd * HEAD_DIM:(hd + 1) * HEAD_DIM]
            logits = _moba_gate_logits(qh, kbar_ref[:, hd * dq:hd * dq + HEAD_DIM], qi)
            qaug_ref[:, hd * dq:hd * dq + HEAD_DIM] = qh
            qaug_ref[:, hd * dq + HEAD_DIM:(hd + 1) * dq] = logits.astype(BF16)
        q_src = qaug_ref
    else:
        q_src = q_ref

    def tile(t):
        return pl.ds(pl.multiple_of(t * ATT_TK, ATT_TK), ATT_TK)

    r = lax.broadcasted_iota(jnp.int32, (ATT_TQ, ATT_TQ), 0)
    c = lax.broadcasted_iota(jnp.int32, (ATT_TQ, ATT_TQ), 1)
    tri = c <= r
    if ATT_TK == ATT_TQ:
        t_diag = qi
        mask = tri
    else:
        t_diag = qi // 2
        odd = (qi % 2) == 1
        mask = jnp.concatenate([tri | odd, tri & odd], axis=1)
    dva = 2 * dv

    def scores(t, hd):
        return lax.dot_general(q_src[:, hd * dq:(hd + 1) * dq], k_ref[tile(t), hd * dq:(hd + 1) * dq], nt,
                               preferred_element_type=F32)

    def weighted_values(p, t, hd):
        return jnp.dot(p.astype(BF16), v_ref[tile(t), hd * dva:(hd + 1) * dva], preferred_element_type=F32)

    heads = range(n_heads)
    for hd in heads:
        s = jnp.where(mask, scores(t_diag, hd), NEG_INF)
        m = jnp.max(s, axis=-1, keepdims=True)
        m_ref[hd] = m
        acc_ref[hd] = weighted_values(jnp.exp(s - m), t_diag, hd)

    def body(t, carry):
        t_next = jnp.minimum(t + 1, t_diag - 1)
        for hd in heads:
            m_cur = mnext_ref[hd]
            p = jnp.exp(s_ref[hd] - m_cur)
            s = scores(t_next, hd)
            s_ref[hd] = s
            mnext_ref[hd] = jnp.maximum(m_cur, jnp.max(s, axis=-1, keepdims=True))
            alpha = jnp.exp(m_ref[hd] - m_cur)
            acc_ref[hd] = alpha * acc_ref[hd] + weighted_values(p, t, hd)
            m_ref[hd] = m_cur
        return carry

    for hd in heads:
        s = scores(0, hd)
        s_ref[hd] = s
        mnext_ref[hd] = jnp.maximum(m_ref[hd], jnp.max(s, axis=-1, keepdims=True))
    lax.fori_loop(0, t_diag, body, 0)
    for hd in heads:
        acc = acc_ref[hd]
        o_ref[:, hd * dv:(hd + 1) * dv] = (acc[:, :dv] / acc[:, dv:]).astype(o_ref.dtype)


def _attention(packed, q_off, q_width, k_off, v_off, n_heads, dq, dv, moba):
    step_rows = ATT_STEP_TILES * ATT_TQ
    nq = SEQ // step_rows
    k_width, v_width = n_heads * dq, n_heads * 2 * dv
    q_blk, k_blk, v_blk = q_off // q_width, k_off // k_width, v_off // v_width
    scratch = [pltpu.VMEM((n_heads, ATT_TQ, 1), F32), pltpu.VMEM((n_heads, ATT_TQ, 1), F32),
               pltpu.VMEM((n_heads, ATT_TQ, 2 * dv), F32), pltpu.VMEM((n_heads, ATT_TQ, ATT_TK), F32)]
    if moba:
        scratch += [pltpu.VMEM((LANES, n_heads * dq), F32), pltpu.VMEM((ATT_TQ, n_heads * dq), BF16)]
    return pl.pallas_call(
        functools.partial(_attn_kernel, n_heads=n_heads, dq=dq, dv=dv, moba=moba),
        grid=(BATCH, nq),
        in_specs=[
            pl.BlockSpec((step_rows, q_width), lambda b, i: (b * nq + i, q_blk)),
            pl.BlockSpec((SEQ, k_width), lambda b, i: (b, k_blk)),
            pl.BlockSpec((SEQ, v_width), lambda b, i: (b, v_blk)),
        ],
        out_specs=pl.BlockSpec((step_rows, n_heads * dv), lambda b, i: (b * nq + i, 0)),
        out_shape=jax.ShapeDtypeStruct((TOKENS, n_heads * dv), BF16),
        scratch_shapes=scratch,
        compiler_params=_params("parallel", "arbitrary"),
        name="moba_attn" if moba else "mla_attn",
    )(packed, packed, packed)


SWA_STEP_BLOCKS = 4


OUT_TN = 512
W_MLA = N_MLA_HEADS * MLA_V_DIM
W_MOBA = N_MOBA_HEADS * HEAD_DIM
W_SWA = N_SWA_HEADS * HEAD_DIM
MIX_WIDTH = W_MLA + W_MOBA + W_SWA
SWA_STEP_ROWS = SWA_STEP_BLOCKS * SWA_WINDOW


def _swa_out_kernel(sink_ref, x_ref, a_ref, b_ref, q_ref, kp_ref, kc_ref, vp_ref, vc_ref, w_ref, o_ref,
                    w16_ref, s_ref, p_ref, m_ref, c_ref):
    step = pl.program_id(0)
    n = step % (SEQ // SWA_STEP_ROWS)
    W = SWA_WINDOW
    G = SWA_GROUP
    nt = (((1,), (1,)), ((), ()))
    blocks = range(SWA_STEP_BLOCKS)
    col_chunks = [slice(c * OUT_TN, (c + 1) * OUT_TN) for c in range(D_MODEL // OUT_TN)]

    @pl.when(step == 0)
    def _():
        def body(r, carry):
            sl = pl.ds(pl.multiple_of(r * NORM_ROWS, NORM_ROWS), NORM_ROWS)
            w16_ref[sl, :] = w_ref[sl, :].astype(BF16)
            return carry

        lax.fori_loop(0, MIX_WIDTH // NORM_ROWS, body, 0)

    r = lax.broadcasted_iota(jnp.int32, (G * W, 2 * W), 0) % W
    c = lax.broadcasted_iota(jnp.int32, (G * W, 2 * W), 1)
    rel = r + W - c
    band = (rel >= 0) & (rel < W)

    def project(lhs, w_rows, cols, first):
        part = jnp.dot(lhs, w16_ref[w_rows, cols], preferred_element_type=F32)
        if first:
            o_ref[:, cols] = x_ref[:, cols] + part
        else:
            o_ref[:, cols] += part

    fillers = [
        [functools.partial(project, a_ref[...], slice(0, W_MLA), cols, True) for cols in col_chunks],
        [functools.partial(project, b_ref[...], slice(W_MLA, W_MLA + W_MOBA), cols, False) for cols in col_chunks],
    ]

    for hk in range(N_SWA_KV_HEADS):
        kcols = slice(hk * HEAD_DIM, (hk + 1) * HEAD_DIM)
        vcols = slice(hk * 2 * HEAD_DIM, (hk + 1) * 2 * HEAD_DIM)
        kk = jnp.concatenate([kp_ref[:, kcols], kc_ref[:, kcols]], axis=0)
        vv = jnp.concatenate([vp_ref[:, vcols], vc_ref[:, vcols]], axis=0)
        sink = jnp.concatenate([jnp.full((W, 1), sink_ref[hk * G + g], F32) for g in range(G)], axis=0)
        fill = fillers[hk]
        for blk in blocks:
            q = jnp.concatenate([q_ref[blk * W:(blk + 1) * W, (hk * G + g) * LANES:(hk * G + g + 1) * LANES]
                                 for g in range(G)], axis=0)
            s = lax.dot_general(q, kk[blk * W:(blk + 2) * W], nt, preferred_element_type=F32)
            mask = band & ((c >= W) | (n > 0)) if blk == 0 else band
            s_ref[blk] = jnp.where(mask, s, NEG_INF)
        fill[0]()
        for blk in blocks:
            m_ref[blk] = jnp.maximum(jnp.max(s_ref[blk], axis=-1, keepdims=True), sink)
        fill[1]()
        for blk in blocks:
            p_ref[blk] = jnp.exp(s_ref[blk] - m_ref[blk]).astype(BF16)
        fill[2]()
        for blk in blocks:
            pv = jnp.dot(p_ref[blk], vv[blk * W:(blk + 2) * W], preferred_element_type=F32)
            o = pv[:, :LANES] / (pv[:, LANES:] + jnp.exp(sink - m_ref[blk]))
            for g in range(G):
                c_ref[blk * W:(blk + 1) * W, (hk * G + g) * LANES:(hk * G + g + 1) * LANES] = (
                    o[g * W:(g + 1) * W].astype(BF16))
        fill[3]()

    for cols in col_chunks:
        project(c_ref[...], slice(W_MLA + W_MOBA, MIX_WIDTH), cols, False)


def _swa_out(sinks, x, a, b, packed, w, layer):
    W = SWA_WINDOW
    nsteps = SEQ // SWA_STEP_ROWS
    kw = N_SWA_KV_HEADS * HEAD_DIM
    q_blk, k_blk, v_blk = P_SWA_Q // W_SWA, P_SWA_K // kw, P_SWA_V // (2 * kw)
    cur = lambda i: (i, 0)

    def prev_row(i):
        return (i // nsteps) * (SEQ // W) + jnp.maximum((i % nsteps) * SWA_STEP_BLOCKS - 1, 0)

    return pl.pallas_call(
        _swa_out_kernel,
        grid=(TOKENS // SWA_STEP_ROWS,),
        in_specs=[
            pl.BlockSpec(memory_space=pltpu.SMEM),
            pl.BlockSpec((SWA_STEP_ROWS, D_MODEL), cur),
            pl.BlockSpec((SWA_STEP_ROWS, W_MLA), cur),
            pl.BlockSpec((SWA_STEP_ROWS, W_MOBA), cur),
            pl.BlockSpec((SWA_STEP_ROWS, W_SWA), lambda i: (i, q_blk)),
            pl.BlockSpec((W, kw), lambda i: (prev_row(i), k_blk)),
            pl.BlockSpec((SWA_STEP_ROWS, kw), lambda i: (i, k_blk)),
            pl.BlockSpec((W, 2 * kw), lambda i: (prev_row(i), v_blk)),
            pl.BlockSpec((SWA_STEP_ROWS, 2 * kw), lambda i: (i, v_blk)),
            pl.BlockSpec((None, MIX_WIDTH, D_MODEL), lambda i: (layer, 0, 0), pipeline_mode=pl.Buffered(1)),
        ],
        out_specs=pl.BlockSpec((SWA_STEP_ROWS, D_MODEL), cur),
        out_shape=jax.ShapeDtypeStruct((TOKENS, D_MODEL), F32),
        scratch_shapes=[pltpu.VMEM((MIX_WIDTH, D_MODEL), BF16),
                        pltpu.VMEM((SWA_STEP_BLOCKS, SWA_GROUP * W, 2 * W), F32),
                        pltpu.VMEM((SWA_STEP_BLOCKS, SWA_GROUP * W, 2 * W), BF16),
                        pltpu.VMEM((SWA_STEP_BLOCKS, SWA_GROUP * W, 1), F32),
                        pltpu.VMEM((SWA_STEP_ROWS, W_SWA), BF16)],
        compiler_params=_params("arbitrary"),
        name="swa_out",
    )(sinks, x, a, b, packed, packed, packed, packed, packed, w)


def _rope_tables():
    def angles(d):
        half = d // 2
        inv_freq = 1.0 / (ROPE_THETA ** (np.arange(half, dtype=np.float64) * (2.0 / d)))
        return np.arange(SEQ, dtype=np.float64)[:, None] * inv_freq[None, :]

    ang = angles(HEAD_DIM)
    cos = np.concatenate([np.cos(ang), np.cos(ang)], axis=1)
    sin = np.concatenate([-np.sin(ang), np.sin(ang)], axis=1)
    angm = angles(MLA_ROPE_DIM)
    zero = np.zeros_like(angm)
    cosm = np.concatenate([np.cos(angm), zero, np.cos(angm), zero], axis=1)
    sinm = np.concatenate([-np.sin(angm), zero, np.sin(angm), zero], axis=1)
    return jnp.asarray(np.concatenate([cos, sin, cosm, sinm], axis=1), dtype=F32)


def _spread_rope_cols(w):
    half = MLA_ROPE_DIM // 2
    zero = jnp.zeros(w.shape[:-1] + (half,), w.dtype)
    return jnp.concatenate([w[..., :half], zero, w[..., half:], zero], axis=-1)


LAYOUT_COLS = 512


def _layout_w_in_kernel(w_ref, o_ref):
    half = MLA_ROPE_DIM // 2
    src = Z_KROPE + 2 * half
    o_ref[0:Z_KROPE + half, :] = w_ref[0:Z_KROPE + half, :].astype(BF16)
    o_ref[Z_KROPE + half:Z_KROPE + 2 * half, :] = jnp.zeros((half, LAYOUT_COLS), BF16)
    o_ref[Z_KROPE + 2 * half:Z_KROPE + 3 * half, :] = w_ref[Z_KROPE + half:src, :].astype(BF16)
    o_ref[Z_KROPE + 3 * half:Z_MQ, :] = jnp.zeros((half, LAYOUT_COLS), BF16)
    n_rest = w_ref.shape[0] - src
    o_ref[Z_MQ:Z_MQ + n_rest, :] = w_ref[src:, :].astype(BF16)
    o_ref[Z_MQ + n_rest:, :] = jnp.zeros((Z_WIDTH - Z_MQ - n_rest, LAYOUT_COLS), BF16)


def _layout_w_in(w):
    wt = jnp.swapaxes(w, 1, 2)
    return pl.pallas_call(
        _layout_w_in_kernel,
        grid=(DEPTH, D_MODEL // LAYOUT_COLS),
        in_specs=[pl.BlockSpec((None, wt.shape[1], LAYOUT_COLS), lambda l, i: (l, 0, i))],
        out_specs=pl.BlockSpec((None, Z_WIDTH, LAYOUT_COLS), lambda l, i: (l, 0, i)),
        out_shape=jax.ShapeDtypeStruct((DEPTH, Z_WIDTH, D_MODEL), BF16),
        compiler_params=_params("parallel", "parallel"),
        name="layout_w_in",
    )(wt)


def _layout_w_uq(w):
    w = w.astype(BF16).reshape(DEPTH, MLA_Q_LORA, N_MLA_HEADS, MLA_NOPE_DIM + MLA_ROPE_DIM)
    w = jnp.concatenate([w[..., :MLA_NOPE_DIM], _spread_rope_cols(w[..., MLA_NOPE_DIM:])], axis=-1)
    return w.reshape(DEPTH, MLA_Q_LORA, N_MLA_HEADS * MLA_QK_PAD)


def kernel(x, ffn1_norm, ffn1_w_gate, ffn1_w_up, ffn1_w_down, attn_norm, w_in, mla_q_norm, mla_w_uq, mla_kv_norm, mla_w_ukv, swa_sinks, w_out, ffn2_norm, ffn2_w_gate, ffn2_w_up, ffn2_w_down, final_norm):
    tabs = _rope_tables()
    x = x.reshape(TOKENS, D_MODEL)
    ffn1_g = ffn1_norm.reshape(DEPTH, 1, D_MODEL)
    ffn2_g = ffn2_norm.reshape(DEPTH, 1, D_MODEL)
    w_in16 = _layout_w_in(w_in)
    w_uq16 = _layout_w_uq(mla_w_uq)
    w_ukv16 = mla_w_ukv.astype(BF16)
    final_g = final_norm.reshape(1, D_MODEL)
    for l in range(DEPTH):
        x = _ffn(x, ffn1_g, ffn1_w_gate, ffn1_w_up, ffn1_w_down, l, final_g)
        packed = _prep(x, attn_norm[l], w_in16, tabs, mla_q_norm[l], mla_kv_norm[l], w_uq16, w_ukv16, l)
        o_mla = _attention(packed, P_MLA_Q, N_MLA_HEADS * MLA_QK_PAD, P_MLA_K, P_MLA_V,
                           N_MLA_HEADS, MLA_QK_PAD, MLA_V_DIM, moba=False)
        o_moba = _attention(packed, P_MOBA_Q, N_MOBA_HEADS * HEAD_DIM, P_MOBA_K, P_MOBA_V,
                            N_MOBA_HEADS, MOBA_QK_AUG, HEAD_DIM, moba=True)
        x = _swa_out(swa_sinks[l], x, o_mla, o_moba, packed, w_out, l)
        x = _ffn(x, ffn2_g, ffn2_w_gate, ffn2_w_up, ffn2_w_down, l, final_g, post_norm=(l == DEPTH - 1))
    return x.reshape(BATCH, SEQ, D_MODEL)
```

```python
import functools

import jax
import jax.numpy as jnp
import numpy as np
from jax import lax
from jax.experimental import pallas as pl
from jax.experimental.pallas import tpu as pltpu

D_MODEL = 2048
BATCH = 2
SEQ = 4096
DEPTH = 2
TOKENS = BATCH * SEQ

HEAD_DIM = 128
N_MLA_HEADS = 4
MLA_Q_LORA = 512
MLA_KV_LORA = 256
MLA_NOPE_DIM = 128
MLA_ROPE_DIM = 64
MLA_V_DIM = 128
MLA_QK_PAD = 256
N_MOBA_HEADS = 4
MOBA_BLOCK = 256
MOBA_TOPK = 3
MOBA_QK_AUG = 256
N_SWA_HEADS = 8
N_SWA_KV_HEADS = 2
SWA_GROUP = N_SWA_HEADS // N_SWA_KV_HEADS
SWA_WINDOW = 128
D_FF = 5632
ROPE_THETA = 10000.0
NORM_EPS = 1e-6

LANES = 128
Z_WIDTH = 4096

Z_CQ = 0
Z_CKV = 512
Z_KROPE = 768
Z_MQ = 896
Z_MK = 1408
Z_MV = 1920
Z_SQ = 2432
Z_SK = 3456
Z_SV = 3712

VMEM_LIMIT = 56 * 1024 * 1024

BF16 = jnp.bfloat16
F32 = jnp.float32
NEG_INF = float("-inf")


def _params(*sem, flags=None):
    return pltpu.CompilerParams(dimension_semantics=sem, vmem_limit_bytes=VMEM_LIMIT, flags=flags)


FFN_TM = 1024
FFN_TF = 256
FFN_TAIL_CHUNKS = 2
FFN_DOWN_TN = 512
NORM_ROWS = 128


def _rmsnorm_rows(x, g):
    ms = jnp.mean(x * x, axis=-1, keepdims=True)
    return x * lax.rsqrt(ms + NORM_EPS) * g


def _norm_into(x_ref, g_ref, h_ref, rows, copy_ref=None):
    g = g_ref[...]

    def body(r, carry):
        sl = pl.ds(pl.multiple_of(r * NORM_ROWS, NORM_ROWS), NORM_ROWS)
        x = x_ref[sl, :]
        h_ref[sl, :] = _rmsnorm_rows(x, g).astype(BF16)
        if copy_ref is not None:
            copy_ref[sl, :] = x
        return carry

    lax.fori_loop(0, rows // NORM_ROWS, body, 0)


def _ffn_act(h, wg, wu):
    gate = jnp.dot(h, wg, preferred_element_type=F32)
    up = jnp.dot(h, wu, preferred_element_type=F32)
    return (gate * jax.nn.sigmoid(gate) * (0.5 * up)).astype(BF16)


def _ffn_down(act, wd_cols, o_ref):
    for c in range(D_MODEL // FFN_DOWN_TN):
        cols = slice(c * FFN_DOWN_TN, (c + 1) * FFN_DOWN_TN)
        o_ref[:, cols] += jnp.dot(act, wd_cols(cols), preferred_element_type=F32)


def _norm_in_place(o_ref, g_ref, rows):
    g = g_ref[...]

    def body(r, carry):
        sl = pl.ds(pl.multiple_of(r * NORM_ROWS, NORM_ROWS), NORM_ROWS)
        o_ref[sl, :] = _rmsnorm_rows(o_ref[sl, :], g)
        return carry

    lax.fori_loop(0, rows // NORM_ROWS, body, 0)


def _ffn_head_kernel(x_ref, g_ref, post_g_ref, wg_ref, wu_ref, wd_ref, o_ref, wg16_ref, wu16_ref, wd16_ref, h_ref,
                     *, post_norm):
    @pl.when(pl.program_id(0) == 0)
    def _():
        _norm_into(x_ref, g_ref, h_ref, FFN_TM, copy_ref=o_ref)

    wg16_ref[...] = wg_ref[...].astype(BF16)
    wu16_ref[...] = wu_ref[...].astype(BF16)
    wd16_ref[...] = wd_ref[...].astype(BF16)
    _ffn_down(_ffn_act(h_ref[...], wg16_ref[...], wu16_ref[...]), lambda cols: wd16_ref[:, cols], o_ref)

    if post_norm:
        @pl.when(pl.program_id(0) == pl.num_programs(0) - 1)
        def _():
            _norm_in_place(o_ref, post_g_ref, FFN_TM)


def _ffn_tail_kernel(x_ref, g_ref, post_g_ref, wg16_ref, wu16_ref, wd16_ref, o_ref, h_ref, *, post_norm):
    @pl.when(pl.program_id(1) == 0)
    def _():
        _norm_into(x_ref, g_ref, h_ref, FFN_TM, copy_ref=o_ref)

    act = jnp.concatenate([_ffn_act(h_ref[...], wg16_ref[c], wu16_ref[c]) for c in range(FFN_TAIL_CHUNKS)], axis=1)
    _ffn_down(act, lambda cols: wd16_ref[:, cols], o_ref)

    if post_norm:
        @pl.when(pl.program_id(1) == pl.num_programs(1) - 1)
        def _():
            _norm_in_place(o_ref, post_g_ref, FFN_TM)


def _ffn(x, g, wg, wu, wd, layer, post_g, post_norm=False):
    n_chunks = D_FF // FFN_TF
    out, wg16, wu16, wd16 = pl.pallas_call(
        functools.partial(_ffn_head_kernel, post_norm=post_norm),
        grid=(n_chunks,),
        in_specs=[
            pl.BlockSpec((FFN_TM, D_MODEL), lambda j: (0, 0)),
            pl.BlockSpec((None, 1, D_MODEL), lambda j: (layer, 0, 0)),
            pl.BlockSpec((1, D_MODEL), lambda j: (0, 0)),
            pl.BlockSpec((None, D_MODEL, FFN_TF), lambda j: (layer, 0, j)),
            pl.BlockSpec((None, D_MODEL, FFN_TF), lambda j: (layer, 0, j)),
            pl.BlockSpec((None, FFN_TF, D_MODEL), lambda j: (layer, j, 0)),
        ],
        out_specs=[
            pl.BlockSpec((FFN_TM, D_MODEL), lambda j: (0, 0)),
            pl.BlockSpec((None, D_MODEL, FFN_TF), lambda j: (j, 0, 0)),
            pl.BlockSpec((None, D_MODEL, FFN_TF), lambda j: (j, 0, 0)),
            pl.BlockSpec((FFN_TF, D_MODEL), lambda j: (j, 0)),
        ],
        out_shape=[
            jax.ShapeDtypeStruct((TOKENS, D_MODEL), F32),
            jax.ShapeDtypeStruct((n_chunks, D_MODEL, FFN_TF), BF16),
            jax.ShapeDtypeStruct((n_chunks, D_MODEL, FFN_TF), BF16),
            jax.ShapeDtypeStruct((D_FF, D_MODEL), BF16),
        ],
        input_output_aliases={0: 0},
        scratch_shapes=[pltpu.VMEM((FFN_TM, D_MODEL), BF16)],
        compiler_params=_params("arbitrary"),
        name="ffn_head",
    )(x, g, post_g, wg, wu, wd)
    return pl.pallas_call(
        functools.partial(_ffn_tail_kernel, post_norm=post_norm),
        grid=(TOKENS // FFN_TM - 1, n_chunks // FFN_TAIL_CHUNKS),
        in_specs=[
            pl.BlockSpec((FFN_TM, D_MODEL), lambda i, j: (i + 1, 0)),
            pl.BlockSpec((None, 1, D_MODEL), lambda i, j: (layer, 0, 0)),
            pl.BlockSpec((1, D_MODEL), lambda i, j: (0, 0)),
            pl.BlockSpec((FFN_TAIL_CHUNKS, D_MODEL, FFN_TF), lambda i, j: (j, 0, 0)),
            pl.BlockSpec((FFN_TAIL_CHUNKS, D_MODEL, FFN_TF), lambda i, j: (j, 0, 0)),
            pl.BlockSpec((FFN_TAIL_CHUNKS * FFN_TF, D_MODEL), lambda i, j: (j, 0)),
        ],
        out_specs=pl.BlockSpec((FFN_TM, D_MODEL), lambda i, j: (i + 1, 0)),
        out_shape=jax.ShapeDtypeStruct((TOKENS, D_MODEL), F32),
        input_output_aliases={0: 0},
        scratch_shapes=[pltpu.VMEM((FFN_TM, D_MODEL), BF16)],
        compiler_params=_params("parallel", "arbitrary"),
        name="ffn_tail",
    )(out, g, post_g, wg16, wu16, wd16)


PROJ_TN = 512
PREP_TM = 512


def _project_into(x_ref, g_ref, w_ref, z_ref, h_ref):
    _norm_into(x_ref, g_ref, h_ref, PREP_TM)
    nt = (((1,), (1,)), ((), ()))
    for c in range(Z_WIDTH // PROJ_TN):
        cols = slice(c * PROJ_TN, (c + 1) * PROJ_TN)
        z_ref[:, cols] = lax.dot_general(h_ref[...], w_ref[cols, :], nt, preferred_element_type=F32)


def _rope(x, cos, sin_signed):
    return x * cos + pltpu.roll(x, LANES // 2, axis=1) * sin_signed


P_MLA_Q = 0
P_MLA_K = P_MLA_Q + N_MLA_HEADS * MLA_QK_PAD
P_MLA_V = P_MLA_K + N_MLA_HEADS * MLA_QK_PAD
P_MOBA_K = P_MLA_V + N_MLA_HEADS * 2 * MLA_V_DIM
P_MOBA_V = P_MOBA_K + N_MOBA_HEADS * MOBA_QK_AUG
P_SWA_Q = P_MOBA_V + N_MOBA_HEADS * 2 * HEAD_DIM
P_MOBA_Q = P_SWA_Q + N_SWA_HEADS * HEAD_DIM
P_SWA_V = P_MOBA_Q + N_MOBA_HEADS * HEAD_DIM
P_SWA_K = P_SWA_V + N_SWA_KV_HEADS * 2 * HEAD_DIM
P_WIDTH = P_SWA_K + N_SWA_KV_HEADS * HEAD_DIM


def _prep_kernel(x_ref, g_ref, w_ref, tab_ref, qn_ref, kvn_ref, wuq_ref, wukv_ref, p_ref, z_ref, h_ref):
    def section(offset, width):
        return p_ref.at[:, offset:offset + width]

    mlaq_ref = section(P_MLA_Q, N_MLA_HEADS * MLA_QK_PAD)
    mlak_ref = section(P_MLA_K, N_MLA_HEADS * MLA_QK_PAD)
    mlav_ref = section(P_MLA_V, N_MLA_HEADS * 2 * MLA_V_DIM)
    mq_ref = section(P_MOBA_Q, N_MOBA_HEADS * HEAD_DIM)
    mk_ref = section(P_MOBA_K, N_MOBA_HEADS * MOBA_QK_AUG)
    mv_ref = section(P_MOBA_V, N_MOBA_HEADS * 2 * HEAD_DIM)
    sq_ref = section(P_SWA_Q, N_SWA_HEADS * HEAD_DIM)
    sk_ref = section(P_SWA_K, N_SWA_KV_HEADS * HEAD_DIM)
    sv_ref = section(P_SWA_V, N_SWA_KV_HEADS * 2 * HEAD_DIM)
    _project_into(x_ref, g_ref, w_ref, z_ref, h_ref)
    cos, sin, cosm, sinm = (tab_ref[:, t * LANES:(t + 1) * LANES] for t in range(4))
    mla_scale = (MLA_NOPE_DIM + MLA_ROPE_DIM) ** -0.5
    scale = HEAD_DIM ** -0.5

    cq = _rmsnorm_rows(z_ref[:, Z_CQ:Z_CQ + MLA_Q_LORA], qn_ref[...]).astype(BF16)
    q = jnp.dot(cq, wuq_ref[...], preferred_element_type=F32)
    for hd in range(N_MLA_HEADS):
        base = hd * MLA_QK_PAD
        mlaq_ref[:, base:base + LANES] = (q[:, base:base + LANES] * mla_scale).astype(BF16)
        pe = _rope(q[:, base + LANES:base + 2 * LANES], cosm, sinm)
        mlaq_ref[:, base + LANES:base + 2 * LANES] = (pe * mla_scale).astype(BF16)

    ones = jnp.ones((PREP_TM, LANES), BF16)
    ckv = _rmsnorm_rows(z_ref[:, Z_CKV:Z_CKV + MLA_KV_LORA], kvn_ref[...]).astype(BF16)
    kv = jnp.dot(ckv, wukv_ref[...], preferred_element_type=F32)
    kpe = _rope(z_ref[:, Z_KROPE:Z_KROPE + LANES], cosm, sinm).astype(BF16)
    for hd in range(N_MLA_HEADS):
        base = hd * MLA_QK_PAD
        mlak_ref[:, base:base + LANES] = kv[:, base:base + LANES].astype(BF16)
        mlak_ref[:, base + LANES:base + 2 * LANES] = kpe
        mlav_ref[:, base:base + LANES] = kv[:, base + LANES:base + 2 * LANES].astype(BF16)
        mlav_ref[:, base + LANES:base + 2 * LANES] = ones

    pos = (pl.program_id(0) % (SEQ // PREP_TM)) * PREP_TM + lax.broadcasted_iota(jnp.int32, (PREP_TM, LANES), 0)
    lane = lax.broadcasted_iota(jnp.int32, (PREP_TM, LANES), 1)
    block_onehot = jnp.where(pos // MOBA_BLOCK == lane, 1.0, 0.0).astype(BF16)
    for hd in range(N_MOBA_HEADS):
        c = hd * LANES
        mq_ref[:, c:c + LANES] = (_rope(z_ref[:, Z_MQ + c:Z_MQ + c + LANES], cos, sin) * scale).astype(BF16)
        mk_ref[:, 2 * c:2 * c + LANES] = _rope(z_ref[:, Z_MK + c:Z_MK + c + LANES], cos, sin).astype(BF16)
        mk_ref[:, 2 * c + LANES:2 * c + 2 * LANES] = block_onehot
        mv_ref[:, 2 * c:2 * c + LANES] = z_ref[:, Z_MV + c:Z_MV + c + LANES].astype(BF16)
        mv_ref[:, 2 * c + LANES:2 * c + 2 * LANES] = ones

    for hd in range(N_SWA_HEADS):
        c = hd * LANES
        sq_ref[:, c:c + LANES] = (_rope(z_ref[:, Z_SQ + c:Z_SQ + c + LANES], cos, sin) * scale).astype(BF16)
    for hd in range(N_SWA_KV_HEADS):
        c = hd * LANES
        sk_ref[:, c:c + LANES] = _rope(z_ref[:, Z_SK + c:Z_SK + c + LANES], cos, sin).astype(BF16)
        sv_ref[:, 2 * c:2 * c + LANES] = z_ref[:, Z_SV + c:Z_SV + c + LANES].astype(BF16)
        sv_ref[:, 2 * c + LANES:2 * c + 2 * LANES] = ones


def _prep(x, g, w_in, tabs, qn, kvn, wuq, wukv, layer):
    nblk = SEQ // PREP_TM
    row = lambda i: (i, 0)
    tab = lambda i: (i % nblk, 0)
    const = lambda i: (0, 0)
    of_layer = lambda i: (layer, 0, 0)
    return pl.pallas_call(
        _prep_kernel,
        grid=(TOKENS // PREP_TM,),
        in_specs=[
            pl.BlockSpec((PREP_TM, D_MODEL), row),
            pl.BlockSpec((1, D_MODEL), const),
            pl.BlockSpec((None, Z_WIDTH, D_MODEL), of_layer, pipeline_mode=pl.Buffered(1)),
            pl.BlockSpec((PREP_TM, 4 * LANES), tab),
            pl.BlockSpec((1, MLA_Q_LORA), const),
            pl.BlockSpec((1, MLA_KV_LORA), const),
            pl.BlockSpec((None, MLA_Q_LORA, N_MLA_HEADS * MLA_QK_PAD), of_layer),
            pl.BlockSpec((None, MLA_KV_LORA, N_MLA_HEADS * MLA_QK_PAD), of_layer),
        ],
        out_specs=pl.BlockSpec((PREP_TM, P_WIDTH), row),
        out_shape=jax.ShapeDtypeStruct((TOKENS, P_WIDTH), BF16),
        scratch_shapes=[pltpu.VMEM((PREP_TM, Z_WIDTH), F32), pltpu.VMEM((PREP_TM, D_MODEL), BF16)],
        compiler_params=_params("parallel"),
        name="mixer_prep",
    )(x, g.reshape(1, D_MODEL), w_in, tabs, qn.reshape(1, -1), kvn.reshape(1, -1), wuq, wukv)


ATT_TQ = MOBA_BLOCK
ATT_TK = 2 * MOBA_BLOCK
ATT_STEP_TILES = 4


def _split_bf16(x):
    hi = x.astype(BF16)
    lo = (x - hi.astype(F32)).astype(BF16)
    return hi, lo


MASKED = -1e30


def _moba_gate_logits(q, kbar, qi):
    nblk = SEQ // MOBA_BLOCK
    nt = (((1,), (1,)), ((), ()))
    kb_hi, kb_lo = _split_bf16(kbar)
    gate = (lax.dot_general(kb_hi, q, nt, preferred_element_type=F32)
            + lax.dot_general(kb_lo, q, nt, preferred_element_type=F32))[:nblk]
    blk = lax.broadcasted_iota(jnp.int32, gate.shape, 0)
    gate = jnp.where(blk < qi, gate, NEG_INF)
    rank = jnp.zeros(gate.shape, jnp.int32)
    for other in range(nblk):
        row = gate[other:other + 1, :]
        beats = (row > gate) | ((row == gate) & (other < blk))
        rank = rank + jnp.where(beats, 1, 0)
    keep = ((rank < MOBA_TOPK) & (blk < qi)) | (blk == qi)
    logit_t = jnp.where(keep, 0.0, MASKED)
    logit_t = jnp.concatenate([logit_t, jnp.full((LANES - nblk, gate.shape[1]), MASKED, F32)], axis=0)
    return logit_t.T


def _attn_kernel(q_ref, k_ref, v_ref, o_ref, m_ref, mnext_ref, acc_ref, s_ref, *moba_scratch,
                 n_heads, dq, dv, moba):
    step = pl.program_id(1)
    if moba:
        kbar_ref, qaug_ref = moba_scratch

        @pl.when(step == 0)
        def _():
            kbar_ref[...] = jnp.zeros_like(kbar_ref)

            def body(b, carry):
                sl = pl.ds(pl.multiple_of(b * MOBA_BLOCK, MOBA_BLOCK), MOBA_BLOCK)
                kbar_ref[pl.ds(b, 1), :] = jnp.mean(k_ref[sl, :].astype(F32), axis=0, keepdims=True)
                return carry

            lax.fori_loop(0, SEQ // MOBA_BLOCK, body, 0)

    for sub in range(ATT_STEP_TILES):
        rows = slice(sub * ATT_TQ, (sub + 1) * ATT_TQ)
        _attn_tile(step * ATT_STEP_TILES + sub, q_ref.at[rows], k_ref, v_ref, o_ref.at[rows],
                   m_ref, mnext_ref, acc_ref, s_ref, *moba_scratch, n_heads=n_heads, dq=dq, dv=dv, moba=moba)


def _attn_tile(qi, q_ref, k_ref, v_ref, o_ref, m_ref, mnext_ref, acc_ref, s_ref, *moba_scratch,
               n_heads, dq, dv, moba):
    nt = (((1,), (1,)), ((), ()))
    if moba:
        kbar_ref, qaug_ref = moba_scratch
        for hd in range(n_heads):
            qh = q_ref[:, hd * HEAD_DIM:(hd + 1) * HEAD_DIM]
            logits = _moba_gate_logits(qh, kbar_ref[:, hd * dq:hd * dq + HEAD_DIM], qi)
            qaug_ref[:, hd * dq:hd * dq + HEAD_DIM] = qh
            qaug_ref[:, hd * dq + HEAD_DIM:(hd + 1) * dq] = logits.astype(BF16)
        q_src = qaug_ref
    else:
        q_src = q_ref

    def tile(t):
        return pl.ds(pl.multiple_of(t * ATT_TK, ATT_TK), ATT_TK)

    r = lax.broadcasted_iota(jnp.int32, (ATT_TQ, ATT_TQ), 0)
    c = lax.broadcasted_iota(jnp.int32, (ATT_TQ, ATT_TQ), 1)
    tri = c <= r
    t_diag = qi // 2
    odd = (qi % 2) == 1
    mask = jnp.concatenate([tri | odd, tri & odd], axis=1)
    dva = 2 * dv

    def scores(t, hd):
        return lax.dot_general(q_src[:, hd * dq:(hd + 1) * dq], k_ref[tile(t), hd * dq:(hd + 1) * dq], nt,
                               preferred_element_type=F32)

    def weighted_values(p, t, hd):
        return jnp.dot(p.astype(BF16), v_ref[tile(t), hd * dva:(hd + 1) * dva], preferred_element_type=F32)

    heads = range(n_heads)
    for hd in heads:
        s = jnp.where(mask, scores(t_diag, hd), NEG_INF)
        m = jnp.max(s, axis=-1, keepdims=True)
        m_ref[hd] = m
        acc_ref[hd] = weighted_values(jnp.exp(s - m), t_diag, hd)

    def body(t, carry):
        t_next = jnp.minimum(t + 1, t_diag - 1)
        for hd in heads:
            m_cur = mnext_ref[hd]
            p = jnp.exp(s_ref[hd] - m_cur)
            s = scores(t_next, hd)
            s_ref[hd] = s
            mnext_ref[hd] = jnp.maximum(m_cur, jnp.max(s, axis=-1, keepdims=True))
            alpha = jnp.exp(m_ref[hd] - m_cur)
            acc_ref[hd] = alpha * acc_ref[hd] + weighted_values(p, t, hd)
            m_ref[hd] = m_cur
        return carry

    for hd in heads:
        s = scores(0, hd)
        s_ref[hd] = s
        mnext_ref[hd] = jnp.maximum(m_ref[hd], jnp.max(s, axis=-1, keepdims=True))
    lax.fori_loop(0, t_diag, body, 0)
    for hd in heads:
        acc = acc_ref[hd]
        o_ref[:, hd * dv:(hd + 1) * dv] = (acc[:, :dv] / acc[:, dv:]).astype(o_ref.dtype)


def _attention(packed, q_off, q_width, k_off, v_off, n_heads, dq, dv, moba):
    step_rows = ATT_STEP_TILES * ATT_TQ
    nq = SEQ // step_rows
    k_width, v_width = n_heads * dq, n_heads * 2 * dv
    q_blk, k_blk, v_blk = q_off // q_width, k_off // k_width, v_off // v_width
    scratch = [pltpu.VMEM((n_heads, ATT_TQ, 1), F32), pltpu.VMEM((n_heads, ATT_TQ, 1), F32),
               pltpu.VMEM((n_heads, ATT_TQ, 2 * dv), F32), pltpu.VMEM((n_heads, ATT_TQ, ATT_TK), F32)]
    if moba:
        scratch += [pltpu.VMEM((LANES, n_heads * dq), F32), pltpu.VMEM((ATT_TQ, n_heads * dq), BF16)]
    return pl.pallas_call(
        functools.partial(_attn_kernel, n_heads=n_heads, dq=dq, dv=dv, moba=moba),
        grid=(BATCH, nq),
        in_specs=[
            pl.BlockSpec((step_rows, q_width), lambda b, i: (b * nq + i, q_blk)),
            pl.BlockSpec((SEQ, k_width), lambda b, i: (b, k_blk)),
            pl.BlockSpec((SEQ, v_width), lambda b, i: (b, v_blk)),
        ],
        out_specs=pl.BlockSpec((step_rows, n_heads * dv), lambda b, i: (b * nq + i, 0)),
        out_shape=jax.ShapeDtypeStruct((TOKENS, n_heads * dv), BF16),
        scratch_shapes=scratch,
        compiler_params=_params("parallel", "arbitrary"),
        name="moba_attn" if moba else "mla_attn",
    )(packed, packed, packed)


SWA_STEP_BLOCKS = 4


OUT_TN = 512
W_MLA = N_MLA_HEADS * MLA_V_DIM
W_MOBA = N_MOBA_HEADS * HEAD_DIM
W_SWA = N_SWA_HEADS * HEAD_DIM
MIX_WIDTH = W_MLA + W_MOBA + W_SWA
SWA_STEP_ROWS = SWA_STEP_BLOCKS * SWA_WINDOW


def _swa_out_kernel(sink_ref, x_ref, a_ref, b_ref, q_ref, kp_ref, kc_ref, vp_ref, vc_ref, w_ref, o_ref,
                    w16_ref, s_ref, p_ref, m_ref, c_ref):
    step = pl.program_id(0)
    n = step % (SEQ // SWA_STEP_ROWS)
    W = SWA_WINDOW
    G = SWA_GROUP
    nt = (((1,), (1,)), ((), ()))
    blocks = range(SWA_STEP_BLOCKS)
    col_chunks = [slice(c * OUT_TN, (c + 1) * OUT_TN) for c in range(D_MODEL // OUT_TN)]

    @pl.when(step == 0)
    def _():
        def body(r, carry):
            sl = pl.ds(pl.multiple_of(r * NORM_ROWS, NORM_ROWS), NORM_ROWS)
            w16_ref[sl, :] = w_ref[sl, :].astype(BF16)
            return carry

        lax.fori_loop(0, MIX_WIDTH // NORM_ROWS, body, 0)

    r = lax.broadcasted_iota(jnp.int32, (G * W, 2 * W), 0) % W
    c = lax.broadcasted_iota(jnp.int32, (G * W, 2 * W), 1)
    rel = r + W - c
    band = (rel >= 0) & (rel < W)

    def project(lhs, w_rows, cols, first):
        part = jnp.dot(lhs, w16_ref[w_rows, cols], preferred_element_type=F32)
        if first:
            o_ref[:, cols] = x_ref[:, cols] + part
        else:
            o_ref[:, cols] += part

    fillers = [
        [functools.partial(project, a_ref[...], slice(0, W_MLA), cols, True) for cols in col_chunks],
        [functools.partial(project, b_ref[...], slice(W_MLA, W_MLA + W_MOBA), cols, False) for cols in col_chunks],
    ]

    for hk in range(N_SWA_KV_HEADS):
        kcols = slice(hk * HEAD_DIM, (hk + 1) * HEAD_DIM)
        vcols = slice(hk * 2 * HEAD_DIM, (hk + 1) * 2 * HEAD_DIM)
        kk = jnp.concatenate([kp_ref[:, kcols], kc_ref[:, kcols]], axis=0)
        vv = jnp.concatenate([vp_ref[:, vcols], vc_ref[:, vcols]], axis=0)
        sink = jnp.concatenate([jnp.full((W, 1), sink_ref[hk * G + g], F32) for g in range(G)], axis=0)
        fill = fillers[hk]
        for blk in blocks:
            q = jnp.concatenate([q_ref[blk * W:(blk + 1) * W, (hk * G + g) * LANES:(hk * G + g + 1) * LANES]
                                 for g in range(G)], axis=0)
            s = lax.dot_general(q, kk[blk * W:(blk + 2) * W], nt, preferred_element_type=F32)
            mask = band & ((c >= W) | (n > 0)) if blk == 0 else band
            s_ref[blk] = jnp.where(mask, s, NEG_INF)
        fill[0]()
        for blk in blocks:
            m_ref[blk] = jnp.maximum(jnp.max(s_ref[blk], axis=-1, keepdims=True), sink)
        fill[1]()
        for blk in blocks:
            p_ref[blk] = jnp.exp(s_ref[blk] - m_ref[blk]).astype(BF16)
        fill[2]()
        for blk in blocks:
            pv = jnp.dot(p_ref[blk], vv[blk * W:(blk + 2) * W], preferred_element_type=F32)
            o = pv[:, :LANES] / (pv[:, LANES:] + jnp.exp(sink - m_ref[blk]))
            for g in range(G):
                c_ref[blk * W:(blk + 1) * W, (hk * G + g) * LANES:(hk * G + g + 1) * LANES] = (
                    o[g * W:(g + 1) * W].astype(BF16))
        fill[3]()

    for cols in col_chunks:
        project(c_ref[...], slice(W_MLA + W_MOBA, MIX_WIDTH), cols, False)


def _swa_out(sinks, x, a, b, packed, w, layer):
    W = SWA_WINDOW
    nsteps = SEQ // SWA_STEP_ROWS
    kw = N_SWA_KV_HEADS * HEAD_DIM
    q_blk, k_blk, v_blk = P_SWA_Q // W_SWA, P_SWA_K // kw, P_SWA_V // (2 * kw)
    cur = lambda i: (i, 0)

    def prev_row(i):
        return (i // nsteps) * (SEQ // W) + jnp.maximum((i % nsteps) * SWA_STEP_BLOCKS - 1, 0)

    return pl.pallas_call(
        _swa_out_kernel,
        grid=(TOKENS // SWA_STEP_ROWS,),
        in_specs=[
            pl.BlockSpec(memory_space=pltpu.SMEM),
            pl.BlockSpec((SWA_STEP_ROWS, D_MODEL), cur),
            pl.BlockSpec((SWA_STEP_ROWS, W_MLA), cur),
            pl.BlockSpec((SWA_STEP_ROWS, W_MOBA), cur),
            pl.BlockSpec((SWA_STEP_ROWS, W_SWA), lambda i: (i, q_blk)),
            pl.BlockSpec((W, kw), lambda i: (prev_row(i), k_blk)),
            pl.BlockSpec((SWA_STEP_ROWS, kw), lambda i: (i, k_blk)),
            pl.BlockSpec((W, 2 * kw), lambda i: (prev_row(i), v_blk)),
            pl.BlockSpec((SWA_STEP_ROWS, 2 * kw), lambda i: (i, v_blk)),
            pl.BlockSpec((None, MIX_WIDTH, D_MODEL), lambda i: (layer, 0, 0), pipeline_mode=pl.Buffered(1)),
        ],
        out_specs=pl.BlockSpec((SWA_STEP_ROWS, D_MODEL), cur),
        out_shape=jax.ShapeDtypeStruct((TOKENS, D_MODEL), F32),
        scratch_shapes=[pltpu.VMEM((MIX_WIDTH, D_MODEL), BF16),
                        pltpu.VMEM((SWA_STEP_BLOCKS, SWA_GROUP * W, 2 * W), F32),
                        pltpu.VMEM((SWA_STEP_BLOCKS, SWA_GROUP * W, 2 * W), BF16),
                        pltpu.VMEM((SWA_STEP_BLOCKS, SWA_GROUP * W, 1), F32),
                        pltpu.VMEM((SWA_STEP_ROWS, W_SWA), BF16)],
        compiler_params=_params("arbitrary"),
        name="swa_out",
    )(sinks, x, a, b, packed, packed, packed, packed, packed, w)


def _rope_tables():
    def angles(d):
        half = d // 2
        inv_freq = 1.0 / (ROPE_THETA ** (np.arange(half, dtype=np.float64) * (2.0 / d)))
        return np.arange(SEQ, dtype=np.float64)[:, None] * inv_freq[None, :]

    ang = angles(HEAD_DIM)
    cos = np.concatenate([np.cos(ang), np.cos(ang)], axis=1)
    sin = np.concatenate([-np.sin(ang), np.sin(ang)], axis=1)
    angm = angles(MLA_ROPE_DIM)
    zero = np.zeros_like(angm)
    cosm = np.concatenate([np.cos(angm), zero, np.cos(angm), zero], axis=1)
    sinm = np.concatenate([-np.sin(angm), zero, np.sin(angm), zero], axis=1)
    return jnp.asarray(np.concatenate([cos, sin, cosm, sinm], axis=1), dtype=F32)


def _spread_rope_cols(w):
    half = MLA_ROPE_DIM // 2
    zero = jnp.zeros(w.shape[:-1] + (half,), w.dtype)
    return jnp.concatenate([w[..., :half], zero, w[..., half:], zero], axis=-1)


LAYOUT_COLS = 512


def _layout_w_in_kernel(w_ref, o_ref):
    half = MLA_ROPE_DIM // 2
    src = Z_KROPE + 2 * half
    o_ref[0:Z_KROPE + half, :] = w_ref[0:Z_KROPE + half, :].astype(BF16)
    o_ref[Z_KROPE + half:Z_KROPE + 2 * half, :] = jnp.zeros((half, LAYOUT_COLS), BF16)
    o_ref[Z_KROPE + 2 * half:Z_KROPE + 3 * half, :] = w_ref[Z_KROPE + half:src, :].astype(BF16)
    o_ref[Z_KROPE + 3 * half:Z_MQ, :] = jnp.zeros((half, LAYOUT_COLS), BF16)
    n_rest = w_ref.shape[0] - src
    o_ref[Z_MQ:Z_MQ + n_rest, :] = w_ref[src:, :].astype(BF16)
    o_ref[Z_MQ + n_rest:, :] = jnp.zeros((Z_WIDTH - Z_MQ - n_rest, LAYOUT_COLS), BF16)


def _layout_w_in(w):
    wt = jnp.swapaxes(w, 1, 2)
    return pl.pallas_call(
        _layout_w_in_kernel,
        grid=(DEPTH, D_MODEL // LAYOUT_COLS),
        in_specs=[pl.BlockSpec((None, wt.shape[1], LAYOUT_COLS), lambda l, i: (l, 0, i))],
        out_specs=pl.BlockSpec((None, Z_WIDTH, LAYOUT_COLS), lambda l, i: (l, 0, i)),
        out_shape=jax.ShapeDtypeStruct((DEPTH, Z_WIDTH, D_MODEL), BF16),
        compiler_params=_params("parallel", "parallel"),
        name="layout_w_in",
    )(wt)


def _layout_w_uq(w):
    w = w.astype(BF16).reshape(DEPTH, MLA_Q_LORA, N_MLA_HEADS, MLA_NOPE_DIM + MLA_ROPE_DIM)
    w = jnp.concatenate([w[..., :MLA_NOPE_DIM], _spread_rope_cols(w[..., MLA_NOPE_DIM:])], axis=-1)
    return w.reshape(DEPTH, MLA_Q_LORA, N_MLA_HEADS * MLA_QK_PAD)


def kernel(x, ffn1_norm, ffn1_w_gate, ffn1_w_up, ffn1_w_down, attn_norm, w_in, mla_q_norm, mla_w_uq, mla_kv_norm, mla_w_ukv, swa_sinks, w_out, ffn2_norm, ffn2_w_gate, ffn2_w_up, ffn2_w_down, final_norm):
    tabs = _rope_tables()
    x = x.reshape(TOKENS, D_MODEL)
    ffn1_g = ffn1_norm.reshape(DEPTH, 1, D_MODEL)
    ffn2_g = ffn2_norm.reshape(DEPTH, 1, D_MODEL)
    w_in16 = _layout_w_in(w_in)
    w_uq16 = _layout_w_uq(mla_w_uq)
    w_ukv16 = mla_w_ukv.astype(BF16)
    final_g = final_norm.reshape(1, D_MODEL)
    for l in range(DEPTH):
        x = _ffn(x, ffn1_g, ffn1_w_gate, ffn1_w_up, ffn1_w_down, l, final_g)
        packed = _prep(x, attn_norm[l], w_in16, tabs, mla_q_norm[l], mla_kv_norm[l], w_uq16, w_ukv16, l)
        o_mla = _attention(packed, P_MLA_Q, N_MLA_HEADS * MLA_QK_PAD, P_MLA_K, P_MLA_V,
                           N_MLA_HEADS, MLA_QK_PAD, MLA_V_DIM, moba=False)
        o_moba = _attention(packed, P_MOBA_Q, N_MOBA_HEADS * HEAD_DIM, P_MOBA_K, P_MOBA_V,
                            N_MOBA_HEADS, MOBA_QK_AUG, HEAD_DIM, moba=True)
        x = _swa_out(swa_sinks[l], x, o_mla, o_moba, packed, w_out, l)
        x = _ffn(x, ffn2_g, ffn2_w_gate, ffn2_w_up, ffn2_w_down, l, final_g, post_norm=(l == DEPTH - 1))
    return x.reshape(BATCH, SEQ, D_MODEL)
```

```python
import functools

import jax
import jax.numpy as jnp
import numpy as np
from jax import lax
from jax.experimental import pallas as pl
from jax.experimental.pallas import tpu as pltpu

D_MODEL = 2048
BATCH = 2
SEQ = 4096
DEPTH = 2
TOKENS = BATCH * SEQ

HEAD_DIM = 128
N_MLA_HEADS = 4
MLA_Q_LORA = 512
MLA_KV_LORA = 256
MLA_NOPE_DIM = 128
MLA_ROPE_DIM = 64
MLA_V_DIM = 128
MLA_QK_PAD = 256
N_MOBA_HEADS = 4
MOBA_BLOCK = 256
MOBA_TOPK = 3
MOBA_QK_AUG = 256
N_SWA_HEADS = 8
N_SWA_KV_HEADS = 2
SWA_GROUP = N_SWA_HEADS // N_SWA_KV_HEADS
SWA_WINDOW = 128
D_FF = 5632
ROPE_THETA = 10000.0
NORM_EPS = 1e-6

LANES = 128

Z_CQ = 0
Z_CKV = Z_CQ + MLA_Q_LORA
Z_KROPE = Z_CKV + MLA_KV_LORA
Z_MQ = Z_KROPE + LANES
Z_MK = Z_MQ + N_MOBA_HEADS * HEAD_DIM
Z_MV = Z_MK + N_MOBA_HEADS * HEAD_DIM
Z_SQ = Z_MV + N_MOBA_HEADS * HEAD_DIM
Z_SK = Z_SQ + N_SWA_HEADS * HEAD_DIM
Z_SV = Z_SK + N_SWA_KV_HEADS * HEAD_DIM
Z_WIDTH = 4096

V7X_VMEM_BYTES = 64 * 1024 * 1024
VMEM_LIMIT = V7X_VMEM_BYTES - 8 * 1024 * 1024

BF16 = jnp.bfloat16
F32 = jnp.float32
NEG_INF = float("-inf")


def _params(*sem):
    return pltpu.CompilerParams(dimension_semantics=sem, vmem_limit_bytes=VMEM_LIMIT)


FFN_TM = 1024
FFN_TF = 256
FFN_TAIL_CHUNKS = 2
FFN_DOWN_TN = 512
NORM_ROWS = 128


def _rmsnorm_rows(x, g):
    ms = jnp.mean(x * x, axis=-1, keepdims=True)
    return x * lax.rsqrt(ms + NORM_EPS) * g


def _norm_into(x_ref, g_ref, h_ref, rows, copy_ref=None):
    g = g_ref[...]

    def body(r, carry):
        sl = pl.ds(pl.multiple_of(r * NORM_ROWS, NORM_ROWS), NORM_ROWS)
        x = x_ref[sl, :]
        h_ref[sl, :] = _rmsnorm_rows(x, g).astype(BF16)
        if copy_ref is not None:
            copy_ref[sl, :] = x
        return carry

    lax.fori_loop(0, rows // NORM_ROWS, body, 0)


def _ffn_act(h, wg, wu):
    gate = jnp.dot(h, wg, preferred_element_type=F32)
    up = jnp.dot(h, wu, preferred_element_type=F32)
    return (gate * jax.nn.sigmoid(gate) * (0.5 * up)).astype(BF16)


def _ffn_down(act, wd_cols, o_ref):
    for c in range(D_MODEL // FFN_DOWN_TN):
        cols = slice(c * FFN_DOWN_TN, (c + 1) * FFN_DOWN_TN)
        o_ref[:, cols] += jnp.dot(act, wd_cols(cols), preferred_element_type=F32)


def _norm_in_place(o_ref, g_ref, rows):
    g = g_ref[...]

    def body(r, carry):
        sl = pl.ds(pl.multiple_of(r * NORM_ROWS, NORM_ROWS), NORM_ROWS)
        o_ref[sl, :] = _rmsnorm_rows(o_ref[sl, :], g)
        return carry

    lax.fori_loop(0, rows // NORM_ROWS, body, 0)


def _ffn_head_kernel(x_ref, g_ref, post_g_ref, wg_ref, wu_ref, wd_ref, o_ref, wg16_ref, wu16_ref, wd16_ref, h_ref,
                     *, post_norm):
    @pl.when(pl.program_id(0) == 0)
    def _():
        _norm_into(x_ref, g_ref, h_ref, FFN_TM, copy_ref=o_ref)

    wg16_ref[...] = wg_ref[...].astype(BF16)
    wu16_ref[...] = wu_ref[...].astype(BF16)
    wd16_ref[...] = wd_ref[...].astype(BF16)
    _ffn_down(_ffn_act(h_ref[...], wg16_ref[...], wu16_ref[...]), lambda cols: wd16_ref[:, cols], o_ref)

    if post_norm:
        @pl.when(pl.program_id(0) == pl.num_programs(0) - 1)
        def _():
            _norm_in_place(o_ref, post_g_ref, FFN_TM)


def _ffn_tail_kernel(x_ref, g_ref, post_g_ref, wg16_ref, wu16_ref, wd16_ref, o_ref, h_ref, *, post_norm):
    @pl.when(pl.program_id(1) == 0)
    def _():
        _norm_into(x_ref, g_ref, h_ref, FFN_TM, copy_ref=o_ref)

    act = jnp.concatenate([_ffn_act(h_ref[...], wg16_ref[c], wu16_ref[c]) for c in range(FFN_TAIL_CHUNKS)], axis=1)
    _ffn_down(act, lambda cols: wd16_ref[:, cols], o_ref)

    if post_norm:
        @pl.when(pl.program_id(1) == pl.num_programs(1) - 1)
        def _():
            _norm_in_place(o_ref, post_g_ref, FFN_TM)


def _ffn(x, g, wg, wu, wd, layer, post_g, post_norm=False):
    n_chunks = D_FF // FFN_TF
    out, wg16, wu16, wd16 = pl.pallas_call(
        functools.partial(_ffn_head_kernel, post_norm=post_norm),
        grid=(n_chunks,),
        in_specs=[
            pl.BlockSpec((FFN_TM, D_MODEL), lambda j: (0, 0)),
            pl.BlockSpec((None, 1, D_MODEL), lambda j: (layer, 0, 0)),
            pl.BlockSpec((1, D_MODEL), lambda j: (0, 0)),
            pl.BlockSpec((None, D_MODEL, FFN_TF), lambda j: (layer, 0, j)),
            pl.BlockSpec((None, D_MODEL, FFN_TF), lambda j: (layer, 0, j)),
            pl.BlockSpec((None, FFN_TF, D_MODEL), lambda j: (layer, j, 0)),
        ],
        out_specs=[
            pl.BlockSpec((FFN_TM, D_MODEL), lambda j: (0, 0)),
            pl.BlockSpec((None, D_MODEL, FFN_TF), lambda j: (j, 0, 0)),
            pl.BlockSpec((None, D_MODEL, FFN_TF), lambda j: (j, 0, 0)),
            pl.BlockSpec((FFN_TF, D_MODEL), lambda j: (j, 0)),
        ],
        out_shape=[
            jax.ShapeDtypeStruct((TOKENS, D_MODEL), F32),
            jax.ShapeDtypeStruct((n_chunks, D_MODEL, FFN_TF), BF16),
            jax.ShapeDtypeStruct((n_chunks, D_MODEL, FFN_TF), BF16),
            jax.ShapeDtypeStruct((D_FF, D_MODEL), BF16),
        ],
        input_output_aliases={0: 0},
        scratch_shapes=[pltpu.VMEM((FFN_TM, D_MODEL), BF16)],
        compiler_params=_params("arbitrary"),
        name="ffn_head",
    )(x, g, post_g, wg, wu, wd)
    return pl.pallas_call(
        functools.partial(_ffn_tail_kernel, post_norm=post_norm),
        grid=(TOKENS // FFN_TM - 1, n_chunks // FFN_TAIL_CHUNKS),
        in_specs=[
            pl.BlockSpec((FFN_TM, D_MODEL), lambda i, j: (i + 1, 0)),
            pl.BlockSpec((None, 1, D_MODEL), lambda i, j: (layer, 0, 0)),
            pl.BlockSpec((1, D_MODEL), lambda i, j: (0, 0)),
            pl.BlockSpec((FFN_TAIL_CHUNKS, D_MODEL, FFN_TF), lambda i, j: (j, 0, 0)),
            pl.BlockSpec((FFN_TAIL_CHUNKS, D_MODEL, FFN_TF), lambda i, j: (j, 0, 0)),
            pl.BlockSpec((FFN_TAIL_CHUNKS * FFN_TF, D_MODEL), lambda i, j: (j, 0)),
        ],
        out_specs=pl.BlockSpec((FFN_TM, D_MODEL), lambda i, j: (i + 1, 0)),
        out_shape=jax.ShapeDtypeStruct((TOKENS, D_MODEL), F32),
        input_output_aliases={0: 0},
        scratch_shapes=[pltpu.VMEM((FFN_TM, D_MODEL), BF16)],
        compiler_params=_params("parallel", "arbitrary"),
        name="ffn_tail",
    )(out, g, post_g, wg16, wu16, wd16)


PROJ_TN = 512
PREP_TM = 512


def _project_into(x_ref, g_ref, w_ref, z_ref, h_ref):
    _norm_into(x_ref, g_ref, h_ref, PREP_TM)
    nt = (((1,), (1,)), ((), ()))
    for c in range(Z_WIDTH // PROJ_TN):
        cols = slice(c * PROJ_TN, (c + 1) * PROJ_TN)
        z_ref[:, cols] = lax.dot_general(h_ref[...], w_ref[cols, :], nt, preferred_element_type=F32)


def _rope(x, cos, sin_signed):
    return x * cos + pltpu.roll(x, LANES // 2, axis=1) * sin_signed


P_MLA_Q = 0
P_MLA_K = P_MLA_Q + N_MLA_HEADS * MLA_QK_PAD
P_MLA_V = P_MLA_K + N_MLA_HEADS * MLA_QK_PAD
P_MOBA_K = P_MLA_V + N_MLA_HEADS * 2 * MLA_V_DIM
P_MOBA_V = P_MOBA_K + N_MOBA_HEADS * MOBA_QK_AUG
P_SWA_Q = P_MOBA_V + N_MOBA_HEADS * 2 * HEAD_DIM
P_MOBA_Q = P_SWA_Q + N_SWA_HEADS * HEAD_DIM
P_SWA_V = P_MOBA_Q + N_MOBA_HEADS * HEAD_DIM
P_SWA_K = P_SWA_V + N_SWA_KV_HEADS * 2 * HEAD_DIM
P_WIDTH = P_SWA_K + N_SWA_KV_HEADS * HEAD_DIM


def _prep_kernel(x_ref, g_ref, w_ref, tab_ref, qn_ref, kvn_ref, wuq_ref, wukv_ref, p_ref, z_ref, h_ref):
    def section(offset, width):
        return p_ref.at[:, offset:offset + width]

    mlaq_ref = section(P_MLA_Q, N_MLA_HEADS * MLA_QK_PAD)
    mlak_ref = section(P_MLA_K, N_MLA_HEADS * MLA_QK_PAD)
    mlav_ref = section(P_MLA_V, N_MLA_HEADS * 2 * MLA_V_DIM)
    mq_ref = section(P_MOBA_Q, N_MOBA_HEADS * HEAD_DIM)
    mk_ref = section(P_MOBA_K, N_MOBA_HEADS * MOBA_QK_AUG)
    mv_ref = section(P_MOBA_V, N_MOBA_HEADS * 2 * HEAD_DIM)
    sq_ref = section(P_SWA_Q, N_SWA_HEADS * HEAD_DIM)
    sk_ref = section(P_SWA_K, N_SWA_KV_HEADS * HEAD_DIM)
    sv_ref = section(P_SWA_V, N_SWA_KV_HEADS * 2 * HEAD_DIM)
    _project_into(x_ref, g_ref, w_ref, z_ref, h_ref)
    cos, sin, cosm, sinm = (tab_ref[:, t * LANES:(t + 1) * LANES] for t in range(4))
    mla_scale = (MLA_NOPE_DIM + MLA_ROPE_DIM) ** -0.5
    scale = HEAD_DIM ** -0.5

    cq = _rmsnorm_rows(z_ref[:, Z_CQ:Z_CQ + MLA_Q_LORA], qn_ref[...]).astype(BF16)
    q = jnp.dot(cq, wuq_ref[...], preferred_element_type=F32)
    for hd in range(N_MLA_HEADS):
        base = hd * MLA_QK_PAD
        mlaq_ref[:, base:base + LANES] = (q[:, base:base + LANES] * mla_scale).astype(BF16)
        pe = _rope(q[:, base + LANES:base + 2 * LANES], cosm, sinm)
        mlaq_ref[:, base + LANES:base + 2 * LANES] = (pe * mla_scale).astype(BF16)

    ones = jnp.ones((PREP_TM, LANES), BF16)
    ckv = _rmsnorm_rows(z_ref[:, Z_CKV:Z_CKV + MLA_KV_LORA], kvn_ref[...]).astype(BF16)
    kv = jnp.dot(ckv, wukv_ref[...], preferred_element_type=F32)
    kpe = _rope(z_ref[:, Z_KROPE:Z_KROPE + LANES], cosm, sinm).astype(BF16)
    for hd in range(N_MLA_HEADS):
        base = hd * MLA_QK_PAD
        mlak_ref[:, base:base + LANES] = kv[:, base:base + LANES].astype(BF16)
        mlak_ref[:, base + LANES:base + 2 * LANES] = kpe
        mlav_ref[:, base:base + LANES] = kv[:, base + LANES:base + 2 * LANES].astype(BF16)
        mlav_ref[:, base + LANES:base + 2 * LANES] = ones

    pos = (pl.program_id(0) % (SEQ // PREP_TM)) * PREP_TM + lax.broadcasted_iota(jnp.int32, (PREP_TM, LANES), 0)
    lane = lax.broadcasted_iota(jnp.int32, (PREP_TM, LANES), 1)
    block_onehot = jnp.where(pos // MOBA_BLOCK == lane, 1.0, 0.0).astype(BF16)
    for hd in range(N_MOBA_HEADS):
        c = hd * LANES
        mq_ref[:, c:c + LANES] = (_rope(z_ref[:, Z_MQ + c:Z_MQ + c + LANES], cos, sin) * scale).astype(BF16)
        mk_ref[:, 2 * c:2 * c + LANES] = _rope(z_ref[:, Z_MK + c:Z_MK + c + LANES], cos, sin).astype(BF16)
        mk_ref[:, 2 * c + LANES:2 * c + 2 * LANES] = block_onehot
        mv_ref[:, 2 * c:2 * c + LANES] = z_ref[:, Z_MV + c:Z_MV + c + LANES].astype(BF16)
        mv_ref[:, 2 * c + LANES:2 * c + 2 * LANES] = ones

    for hd in range(N_SWA_HEADS):
        c = hd * LANES
        sq_ref[:, c:c + LANES] = (_rope(z_ref[:, Z_SQ + c:Z_SQ + c + LANES], cos, sin) * scale).astype(BF16)
    for hd in range(N_SWA_KV_HEADS):
        c = hd * LANES
        sk_ref[:, c:c + LANES] = _rope(z_ref[:, Z_SK + c:Z_SK + c + LANES], cos, sin).astype(BF16)
        sv_ref[:, 2 * c:2 * c + LANES] = z_ref[:, Z_SV + c:Z_SV + c + LANES].astype(BF16)
        sv_ref[:, 2 * c + LANES:2 * c + 2 * LANES] = ones


def _prep(x, g, w_in, tabs, qn, kvn, wuq, wukv, layer):
    nblk = SEQ // PREP_TM
    row = lambda i: (i, 0)
    tab = lambda i: (i % nblk, 0)
    const = lambda i: (0, 0)
    of_layer = lambda i: (layer, 0, 0)
    return pl.pallas_call(
        _prep_kernel,
        grid=(TOKENS // PREP_TM,),
        in_specs=[
            pl.BlockSpec((PREP_TM, D_MODEL), row),
            pl.BlockSpec((1, D_MODEL), const),
            pl.BlockSpec((None, Z_WIDTH, D_MODEL), of_layer, pipeline_mode=pl.Buffered(1)),
            pl.BlockSpec((PREP_TM, 4 * LANES), tab),
            pl.BlockSpec((1, MLA_Q_LORA), const),
            pl.BlockSpec((1, MLA_KV_LORA), const),
            pl.BlockSpec((None, MLA_Q_LORA, N_MLA_HEADS * MLA_QK_PAD), of_layer),
            pl.BlockSpec((None, MLA_KV_LORA, N_MLA_HEADS * MLA_QK_PAD), of_layer),
        ],
        out_specs=pl.BlockSpec((PREP_TM, P_WIDTH), row),
        out_shape=jax.ShapeDtypeStruct((TOKENS, P_WIDTH), BF16),
        scratch_shapes=[pltpu.VMEM((PREP_TM, Z_WIDTH), F32), pltpu.VMEM((PREP_TM, D_MODEL), BF16)],
        compiler_params=_params("parallel"),
        name="mixer_prep",
    )(x, g.reshape(1, D_MODEL), w_in, tabs, qn.reshape(1, -1), kvn.reshape(1, -1), wuq, wukv)


ATT_TQ = MOBA_BLOCK
ATT_TK = 2 * MOBA_BLOCK
ATT_STEP_TILES = 4


def _split_bf16(x):
    hi = x.astype(BF16)
    lo = (x - hi.astype(F32)).astype(BF16)
    return hi, lo


MASKED = -1e30


def _moba_gate_logits(q, kbar, qi):
    nblk = SEQ // MOBA_BLOCK
    nt = (((1,), (1,)), ((), ()))
    kb_hi, kb_lo = _split_bf16(kbar)
    gate = (lax.dot_general(kb_hi, q, nt, preferred_element_type=F32)
            + lax.dot_general(kb_lo, q, nt, preferred_element_type=F32))[:nblk]
    blk = lax.broadcasted_iota(jnp.int32, gate.shape, 0)
    gate = jnp.where(blk < qi, gate, NEG_INF)
    rank = jnp.zeros(gate.shape, jnp.int32)
    for other in range(nblk):
        row = gate[other:other + 1, :]
        beats = (row > gate) | ((row == gate) & (other < blk))
        rank = rank + jnp.where(beats, 1, 0)
    keep = ((rank < MOBA_TOPK) & (blk < qi)) | (blk == qi)
    logit_t = jnp.where(keep, 0.0, MASKED)
    logit_t = jnp.concatenate([logit_t, jnp.full((LANES - nblk, gate.shape[1]), MASKED, F32)], axis=0)
    return logit_t.T


def _attn_kernel(q_ref, k_ref, v_ref, o_ref, m_ref, mnext_ref, acc_ref, s_ref, *moba_scratch,
                 n_heads, dq, dv, moba):
    step = pl.program_id(1)
    if moba:
        kbar_ref, qaug_ref = moba_scratch

        @pl.when(step == 0)
        def _():
            kbar_ref[...] = jnp.zeros_like(kbar_ref)

            def body(b, carry):
                sl = pl.ds(pl.multiple_of(b * MOBA_BLOCK, MOBA_BLOCK), MOBA_BLOCK)
                kbar_ref[pl.ds(b, 1), :] = jnp.mean(k_ref[sl, :].astype(F32), axis=0, keepdims=True)
                return carry

            lax.fori_loop(0, SEQ // MOBA_BLOCK, body, 0)

    for sub in range(ATT_STEP_TILES):
        rows = slice(sub * ATT_TQ, (sub + 1) * ATT_TQ)
        _attn_tile(step * ATT_STEP_TILES + sub, q_ref.at[rows], k_ref, v_ref, o_ref.at[rows],
                   m_ref, mnext_ref, acc_ref, s_ref, *moba_scratch, n_heads=n_heads, dq=dq, dv=dv, moba=moba)


def _attn_tile(qi, q_ref, k_ref, v_ref, o_ref, m_ref, mnext_ref, acc_ref, s_ref, *moba_scratch,
               n_heads, dq, dv, moba):
    nt = (((1,), (1,)), ((), ()))
    if moba:
        kbar_ref, qaug_ref = moba_scratch
        for hd in range(n_heads):
            qh = q_ref[:, hd * HEAD_DIM:(hd + 1) * HEAD_DIM]
            logits = _moba_gate_logits(qh, kbar_ref[:, hd * dq:hd * dq + HEAD_DIM], qi)
            qaug_ref[:, hd * dq:hd * dq + HEAD_DIM] = qh
            qaug_ref[:, hd * dq + HEAD_DIM:(hd + 1) * dq] = logits.astype(BF16)
        q_src = qaug_ref
    else:
        q_src = q_ref

    def tile(t):
        return pl.ds(pl.multiple_of(t * ATT_TK, ATT_TK), ATT_TK)

    r = lax.broadcasted_iota(jnp.int32, (ATT_TQ, ATT_TQ), 0)
    c = lax.broadcasted_iota(jnp.int32, (ATT_TQ, ATT_TQ), 1)
    tri = c <= r
    t_diag = qi // 2
    odd = (qi % 2) == 1
    mask = jnp.concatenate([tri | odd, tri & odd], axis=1)
    dva = 2 * dv

    def scores(t, hd):
        return lax.dot_general(q_src[:, hd * dq:(hd + 1) * dq], k_ref[tile(t), hd * dq:(hd + 1) * dq], nt,
                               preferred_element_type=F32)

    def weighted_values(p, t, hd):
        return jnp.dot(p.astype(BF16), v_ref[tile(t), hd * dva:(hd + 1) * dva], preferred_element_type=F32)

    heads = range(n_heads)
    for hd in heads:
        s = jnp.where(mask, scores(t_diag, hd), NEG_INF)
        m = jnp.max(s, axis=-1, keepdims=True)
        m_ref[hd] = m
        acc_ref[hd] = weighted_values(jnp.exp(s - m), t_diag, hd)

    def body(t, carry):
        t_next = jnp.minimum(t + 1, t_diag - 1)
        for hd in heads:
            m_cur = mnext_ref[hd]
            p = jnp.exp(s_ref[hd] - m_cur)
            s = scores(t_next, hd)
            alpha = jnp.exp(m_ref[hd] - m_cur)
            acc_ref[hd] = alpha * acc_ref[hd] + weighted_values(p, t, hd)
            m_ref[hd] = m_cur
            s_ref[hd] = s
            mnext_ref[hd] = jnp.maximum(m_cur, jnp.max(s, axis=-1, keepdims=True))
        return carry

    for hd in heads:
        s = scores(0, hd)
        s_ref[hd] = s
        mnext_ref[hd] = jnp.maximum(m_ref[hd], jnp.max(s, axis=-1, keepdims=True))
    lax.fori_loop(0, t_diag, body, 0)
    for hd in heads:
        acc = acc_ref[hd]
        o_ref[:, hd * dv:(hd + 1) * dv] = (acc[:, :dv] / acc[:, dv:]).astype(o_ref.dtype)


def _attention(packed, q_off, q_width, k_off, v_off, n_heads, dq, dv, moba):
    step_rows = ATT_STEP_TILES * ATT_TQ
    nq = SEQ // step_rows
    k_width, v_width = n_heads * dq, n_heads * 2 * dv
    q_blk, k_blk, v_blk = q_off // q_width, k_off // k_width, v_off // v_width
    scratch = [pltpu.VMEM((n_heads, ATT_TQ, 1), F32), pltpu.VMEM((n_heads, ATT_TQ, 1), F32),
               pltpu.VMEM((n_heads, ATT_TQ, 2 * dv), F32), pltpu.VMEM((n_heads, ATT_TQ, ATT_TK), F32)]
    if moba:
        scratch += [pltpu.VMEM((LANES, n_heads * dq), F32), pltpu.VMEM((ATT_TQ, n_heads * dq), BF16)]
    return pl.pallas_call(
        functools.partial(_attn_kernel, n_heads=n_heads, dq=dq, dv=dv, moba=moba),
        grid=(BATCH, nq),
        in_specs=[
            pl.BlockSpec((step_rows, q_width), lambda b, i: (b * nq + i, q_blk)),
            pl.BlockSpec((SEQ, k_width), lambda b, i: (b, k_blk)),
            pl.BlockSpec((SEQ, v_width), lambda b, i: (b, v_blk)),
        ],
        out_specs=pl.BlockSpec((step_rows, n_heads * dv), lambda b, i: (b * nq + i, 0)),
        out_shape=jax.ShapeDtypeStruct((TOKENS, n_heads * dv), BF16),
        scratch_shapes=scratch,
        compiler_params=_params("parallel", "arbitrary"),
        name="moba_attn" if moba else "mla_attn",
    )(packed, packed, packed)


SWA_STEP_BLOCKS = 4


OUT_TN = 512
W_MLA = N_MLA_HEADS * MLA_V_DIM
W_MOBA = N_MOBA_HEADS * HEAD_DIM
W_SWA = N_SWA_HEADS * HEAD_DIM
MIX_WIDTH = W_MLA + W_MOBA + W_SWA
SWA_STEP_ROWS = SWA_STEP_BLOCKS * SWA_WINDOW


def _swa_out_kernel(sink_ref, x_ref, a_ref, b_ref, q_ref, kp_ref, kc_ref, vp_ref, vc_ref, w_ref, o_ref,
                    w16_ref, s_ref, p_ref, m_ref, c_ref):
    step = pl.program_id(0)
    n = step % (SEQ // SWA_STEP_ROWS)
    W = SWA_WINDOW
    G = SWA_GROUP
    nt = (((1,), (1,)), ((), ()))
    blocks = range(SWA_STEP_BLOCKS)
    col_chunks = [slice(c * OUT_TN, (c + 1) * OUT_TN) for c in range(D_MODEL // OUT_TN)]

    @pl.when(step == 0)
    def _():
        def body(r, carry):
            sl = pl.ds(pl.multiple_of(r * NORM_ROWS, NORM_ROWS), NORM_ROWS)
            w16_ref[sl, :] = w_ref[sl, :].astype(BF16)
            return carry

        lax.fori_loop(0, MIX_WIDTH // NORM_ROWS, body, 0)

    r = lax.broadcasted_iota(jnp.int32, (G * W, 2 * W), 0) % W
    c = lax.broadcasted_iota(jnp.int32, (G * W, 2 * W), 1)
    rel = r + W - c
    band = (rel >= 0) & (rel < W)

    def project(lhs, w_rows, cols, first):
        part = jnp.dot(lhs, w16_ref[w_rows, cols], preferred_element_type=F32)
        if first:
            o_ref[:, cols] = x_ref[:, cols] + part
        else:
            o_ref[:, cols] += part

    fillers = [
        [functools.partial(project, a_ref[...], slice(0, W_MLA), cols, True) for cols in col_chunks],
        [functools.partial(project, b_ref[...], slice(W_MLA, W_MLA + W_MOBA), cols, False) for cols in col_chunks],
    ]

    for hk in range(N_SWA_KV_HEADS):
        kcols = slice(hk * HEAD_DIM, (hk + 1) * HEAD_DIM)
        vcols = slice(hk * 2 * HEAD_DIM, (hk + 1) * 2 * HEAD_DIM)
        kk = jnp.concatenate([kp_ref[:, kcols], kc_ref[:, kcols]], axis=0)
        vv = jnp.concatenate([vp_ref[:, vcols], vc_ref[:, vcols]], axis=0)
        sink = jnp.concatenate([jnp.full((W, 1), sink_ref[hk * G + g], F32) for g in range(G)], axis=0)
        fill = fillers[hk]
        for blk in blocks:
            q = jnp.concatenate([q_ref[blk * W:(blk + 1) * W, (hk * G + g) * LANES:(hk * G + g + 1) * LANES]
                                 for g in range(G)], axis=0)
            s = lax.dot_general(q, kk[blk * W:(blk + 2) * W], nt, preferred_element_type=F32)
            mask = band & ((c >= W) | (n > 0)) if blk == 0 else band
            s_ref[blk] = jnp.where(mask, s, NEG_INF)
        fill[0]()
        for blk in blocks:
            m_ref[blk] = jnp.maximum(jnp.max(s_ref[blk], axis=-1, keepdims=True), sink)
        fill[1]()
        for blk in blocks:
            p_ref[blk] = jnp.exp(s_ref[blk] - m_ref[blk]).astype(BF16)
        fill[2]()
        for blk in blocks:
            pv = jnp.dot(p_ref[blk], vv[blk * W:(blk + 2) * W], preferred_element_type=F32)
            o = pv[:, :LANES] / (pv[:, LANES:] + jnp.exp(sink - m_ref[blk]))
            for g in range(G):
                c_ref[blk * W:(blk + 1) * W, (hk * G + g) * LANES:(hk * G + g + 1) * LANES] = (
                    o[g * W:(g + 1) * W].astype(BF16))
        fill[3]()

    for cols in col_chunks:
        project(c_ref[...], slice(W_MLA + W_MOBA, MIX_WIDTH), cols, False)


def _swa_out(sinks, x, a, b, packed, w, layer):
    W = SWA_WINDOW
    nsteps = SEQ // SWA_STEP_ROWS
    kw = N_SWA_KV_HEADS * HEAD_DIM
    q_blk, k_blk, v_blk = P_SWA_Q // W_SWA, P_SWA_K // kw, P_SWA_V // (2 * kw)
    cur = lambda i: (i, 0)

    def prev_row(i):
        return (i // nsteps) * (SEQ // W) + jnp.maximum((i % nsteps) * SWA_STEP_BLOCKS - 1, 0)

    return pl.pallas_call(
        _swa_out_kernel,
        grid=(TOKENS // SWA_STEP_ROWS,),
        in_specs=[
            pl.BlockSpec(memory_space=pltpu.SMEM),
            pl.BlockSpec((SWA_STEP_ROWS, D_MODEL), cur),
            pl.BlockSpec((SWA_STEP_ROWS, W_MLA), cur),
            pl.BlockSpec((SWA_STEP_ROWS, W_MOBA), cur),
            pl.BlockSpec((SWA_STEP_ROWS, W_SWA), lambda i: (i, q_blk)),
            pl.BlockSpec((W, kw), lambda i: (prev_row(i), k_blk)),
            pl.BlockSpec((SWA_STEP_ROWS, kw), lambda i: (i, k_blk)),
            pl.BlockSpec((W, 2 * kw), lambda i: (prev_row(i), v_blk)),
            pl.BlockSpec((SWA_STEP_ROWS, 2 * kw), lambda i: (i, v_blk)),
            pl.BlockSpec((None, MIX_WIDTH, D_MODEL), lambda i: (layer, 0, 0), pipeline_mode=pl.Buffered(1)),
        ],
        out_specs=pl.BlockSpec((SWA_STEP_ROWS, D_MODEL), cur),
        out_shape=jax.ShapeDtypeStruct((TOKENS, D_MODEL), F32),
        scratch_shapes=[pltpu.VMEM((MIX_WIDTH, D_MODEL), BF16),
                        pltpu.VMEM((SWA_STEP_BLOCKS, SWA_GROUP * W, 2 * W), F32),
                        pltpu.VMEM((SWA_STEP_BLOCKS, SWA_GROUP * W, 2 * W), BF16),
                        pltpu.VMEM((SWA_STEP_BLOCKS, SWA_GROUP * W, 1), F32),
                        pltpu.VMEM((SWA_STEP_ROWS, W_SWA), BF16)],
        compiler_params=_params("arbitrary"),
        name="swa_out",
    )(sinks, x, a, b, packed, packed, packed, packed, packed, w)


def _rope_tables():
    def angles(d):
        half = d // 2
        inv_freq = 1.0 / (ROPE_THETA ** (np.arange(half, dtype=np.float64) * (2.0 / d)))
        return np.arange(SEQ, dtype=np.float64)[:, None] * inv_freq[None, :]

    ang = angles(HEAD_DIM)
    cos = np.concatenate([np.cos(ang), np.cos(ang)], axis=1)
    sin = np.concatenate([-np.sin(ang), np.sin(ang)], axis=1)
    angm = angles(MLA_ROPE_DIM)
    zero = np.zeros_like(angm)
    cosm = np.concatenate([np.cos(angm), zero, np.cos(angm), zero], axis=1)
    sinm = np.concatenate([-np.sin(angm), zero, np.sin(angm), zero], axis=1)
    return jnp.asarray(np.concatenate([cos, sin, cosm, sinm], axis=1), dtype=F32)


def _spread_rope_cols(w):
    half = MLA_ROPE_DIM // 2
    zero = jnp.zeros(w.shape[:-1] + (half,), w.dtype)
    return jnp.concatenate([w[..., :half], zero, w[..., half:], zero], axis=-1)


LAYOUT_COLS = 512


def _layout_w_in_kernel(w_ref, o_ref):
    half = MLA_ROPE_DIM // 2
    src = Z_KROPE + 2 * half
    o_ref[0:Z_KROPE + half, :] = w_ref[0:Z_KROPE + half, :].astype(BF16)
    o_ref[Z_KROPE + half:Z_KROPE + 2 * half, :] = jnp.zeros((half, LAYOUT_COLS), BF16)
    o_ref[Z_KROPE + 2 * half:Z_KROPE + 3 * half, :] = w_ref[Z_KROPE + half:src, :].astype(BF16)
    o_ref[Z_KROPE + 3 * half:Z_MQ, :] = jnp.zeros((half, LAYOUT_COLS), BF16)
    n_rest = w_ref.shape[0] - src
    o_ref[Z_MQ:Z_MQ + n_rest, :] = w_ref[src:, :].astype(BF16)
    o_ref[Z_MQ + n_rest:, :] = jnp.zeros((Z_WIDTH - Z_MQ - n_rest, LAYOUT_COLS), BF16)


def _layout_w_in(w):
    wt = jnp.swapaxes(w, 1, 2)
    return pl.pallas_call(
        _layout_w_in_kernel,
        grid=(DEPTH, D_MODEL // LAYOUT_COLS),
        in_specs=[pl.BlockSpec((None, wt.shape[1], LAYOUT_COLS), lambda l, i: (l, 0, i))],
        out_specs=pl.BlockSpec((None, Z_WIDTH, LAYOUT_COLS), lambda l, i: (l, 0, i)),
        out_shape=jax.ShapeDtypeStruct((DEPTH, Z_WIDTH, D_MODEL), BF16),
        compiler_params=_params("parallel", "parallel"),
        name="layout_w_in",
    )(wt)


def _layout_w_uq(w):
    w = w.astype(BF16).reshape(DEPTH, MLA_Q_LORA, N_MLA_HEADS, MLA_NOPE_DIM + MLA_ROPE_DIM)
    w = jnp.concatenate([w[..., :MLA_NOPE_DIM], _spread_rope_cols(w[..., MLA_NOPE_DIM:])], axis=-1)
    return w.reshape(DEPTH, MLA_Q_LORA, N_MLA_HEADS * MLA_QK_PAD)


def kernel(x, ffn1_norm, ffn1_w_gate, ffn1_w_up, ffn1_w_down, attn_norm, w_in, mla_q_norm, mla_w_uq, mla_kv_norm, mla_w_ukv, swa_sinks, w_out, ffn2_norm, ffn2_w_gate, ffn2_w_up, ffn2_w_down, final_norm):
    assert x.shape == (BATCH, SEQ, D_MODEL) and x.dtype == F32, (x.shape, x.dtype)
    assert w_in.shape == (DEPTH, D_MODEL, Z_SV + N_SWA_KV_HEADS * HEAD_DIM - LANES // 2), w_in.shape
    assert ffn1_w_gate.shape == ffn2_w_gate.shape == (DEPTH, D_MODEL, D_FF), ffn1_w_gate.shape
    tabs = _rope_tables()
    x = x.reshape(TOKENS, D_MODEL)
    ffn1_g = ffn1_norm.reshape(DEPTH, 1, D_MODEL)
    ffn2_g = ffn2_norm.reshape(DEPTH, 1, D_MODEL)
    w_in16 = _layout_w_in(w_in)
    w_uq16 = _layout_w_uq(mla_w_uq)
    w_ukv16 = mla_w_ukv.astype(BF16)
    final_g = final_norm.reshape(1, D_MODEL)
    for l in range(DEPTH):
        x = _ffn(x, ffn1_g, ffn1_w_gate, ffn1_w_up, ffn1_w_down, l, final_g)
        packed = _prep(x, attn_norm[l], w_in16, tabs, mla_q_norm[l], mla_kv_norm[l], w_uq16, w_ukv16, l)
        o_mla = _attention(packed, P_MLA_Q, N_MLA_HEADS * MLA_QK_PAD, P_MLA_K, P_MLA_V,
                           N_MLA_HEADS, MLA_QK_PAD, MLA_V_DIM, moba=False)
        o_moba = _attention(packed, P_MOBA_Q, N_MOBA_HEADS * HEAD_DIM, P_MOBA_K, P_MOBA_V,
                            N_MOBA_HEADS, MOBA_QK_AUG, HEAD_DIM, moba=True)
        x = _swa_out(swa_sinks[l], x, o_mla, o_moba, packed, w_out, l)
        x = _ffn(x, ffn2_g, ffn2_w_gate, ffn2_w_up, ffn2_w_down, l, final_g, post_norm=(l == DEPTH - 1))
    return x.reshape(BATCH, SEQ, D_MODEL)
```
